```python
import math
import jax, jax.numpy as jnp
from jax import lax
import numpy as np

D_MODEL = 1024
BATCH = 8
SEQ = 4096
DEPTH = 4

GRID_W = 64
NUM_MIXERS = 2
HEAD_DIM = 64
MIX_WIDTH = D_MODEL
A_HEADS = MIX_WIDTH // HEAD_DIM
A_WIN_ROWS = 8
A_WIN_COLS = 16
B_Q_HEADS = MIX_WIDTH // HEAD_DIM
B_KV_HEADS = 4
B_QBLK = 128
ROPE_THETA = 10000.0
ROPE_SECTION = HEAD_DIM // 2
PLE_DIM = 256
NORM_EPS = 1e-6

kernel_name = "hybrid_natten_gqa_axial_rope_encoder"


def rms_norm(x, g):
    xf = x.astype(jnp.float32)
    y = xf * lax.rsqrt(jnp.mean(xf * xf, axis=-1, keepdims=True) + NORM_EPS)
    return (y * g.astype(jnp.float32)).astype(x.dtype)


def _na_indices(rows):
    win_r = min(A_WIN_ROWS, rows)
    win_c = A_WIN_COLS
    r = np.arange(rows)
    c = np.arange(GRID_W)
    rs = np.clip(r - win_r // 2, 0, rows - win_r)
    cs = np.clip(c - win_c // 2, 0, GRID_W - win_c)
    key_r = rs[:, None] + np.arange(win_r)[None, :]
    key_c = cs[:, None] + np.arange(win_c)[None, :]
    idx = key_r[:, None, :, None] * GRID_W + key_c[None, :, None, :]
    idx = idx.reshape(rows, GRID_W, win_r * win_c).astype(np.int32)
    dr = (key_r - r[:, None] + (A_WIN_ROWS - 1)).astype(np.int32)
    dc = (key_c - c[:, None] + (A_WIN_COLS - 1)).astype(np.int32)
    return idx, dr, dc


def mixer_a(h, w_in, rpb, w_out):
    b, s, _ = h.shape
    rows = s // GRID_W
    idx_np, dr, dc = _na_indices(rows)
    nk = idx_np.shape[-1]
    proj = h @ w_in
    q, k, v, gate = jnp.split(proj, 4, axis=-1)
    to_heads = lambda t: t.reshape(b, s, A_HEADS, HEAD_DIM).transpose(0, 2, 1, 3)
    q, k, v = to_heads(q), to_heads(k), to_heads(v)
    bias = rpb[:, dr[:, None, :, None], dc[None, :, None, :]]
    bias = bias.reshape(A_HEADS, rows, GRID_W, nk).transpose(1, 0, 2, 3)
    q_blk = q.reshape(b, A_HEADS, rows, GRID_W, HEAD_DIM).transpose(2, 0, 1, 3, 4)
    idx = jnp.asarray(idx_np)
    scale = 1.0 / math.sqrt(HEAD_DIM)

    def row_block(args):
        qb, ib, bb = args
        flat = ib.reshape(-1)
        kg = jnp.take(k, flat, axis=2).reshape(b, A_HEADS, GRID_W, nk, HEAD_DIM)
        vg = jnp.take(v, flat, axis=2).reshape(b, A_HEADS, GRID_W, nk, HEAD_DIM)
        sc = jnp.einsum('bhqd,bhqnd->bhqn', qb, kg).astype(jnp.float32) * scale
        sc = sc + bb.astype(jnp.float32)[None]
        pr = jax.nn.softmax(sc, axis=-1).astype(vg.dtype)
        return jnp.einsum('bhqn,bhqnd->bhqd', pr, vg)

    o = lax.map(row_block, (q_blk, idx, bias))
    o = o.transpose(1, 0, 3, 2, 4).reshape(b, s, MIX_WIDTH)
    return (o * jax.nn.silu(gate)) @ w_out


def _axial_rope(seq):
    t = jnp.arange(seq)
    row = (t // GRID_W).astype(jnp.float32)
    col = (t % GRID_W).astype(jnp.float32)
    n_freq = ROPE_SECTION // 2
    inv = jnp.power(ROPE_THETA, -jnp.arange(n_freq, dtype=jnp.float32) * 2.0 / ROPE_SECTION)
    ang_r = row[:, None] * inv[None, :]
    ang_c = col[:, None] * inv[None, :]
    return jnp.cos(ang_r), jnp.sin(ang_r), jnp.cos(ang_c), jnp.sin(ang_c)


def _rotate(x, cos, sin):
    x1, x2 = jnp.split(x, 2, axis=-1)
    return jnp.concatenate([x1 * cos - x2 * sin, x2 * cos + x1 * sin], axis=-1)


def _apply_axial_rope(x, tabs):
    cos_r, sin_r, cos_c, sin_c = tabs
    xf = x.astype(jnp.float32)
    xr = _rotate(xf[..., :ROPE_SECTION], cos_r, sin_r)
    xc = _rotate(xf[..., ROPE_SECTION:], cos_c, sin_c)
    return jnp.concatenate([xr, xc], axis=-1).astype(x.dtype)


def mixer_b(h, w_in, q_norm_g, k_norm_g, w_out):
    b, s, _ = h.shape
    group = B_Q_HEADS // B_KV_HEADS
    kv_w = B_KV_HEADS * HEAD_DIM
    proj = h @ w_in
    q, k, v, gate = jnp.split(proj, [MIX_WIDTH, MIX_WIDTH + kv_w, MIX_WIDTH + 2 * kv_w], axis=-1)
    q = rms_norm(q.reshape(b, s, B_Q_HEADS, HEAD_DIM), q_norm_g).transpose(0, 2, 1, 3)
    k = rms_norm(k.reshape(b, s, B_KV_HEADS, HEAD_DIM), k_norm_g).transpose(0, 2, 1, 3)
    v = v.reshape(b, s, B_KV_HEADS, HEAD_DIM).transpose(0, 2, 1, 3)
    tabs = _axial_rope(s)
    q = _apply_axial_rope(q, tabs)
    k = _apply_axial_rope(k, tabs)
    nb = s // B_QBLK
    q_blk = q.reshape(b, B_KV_HEADS, group, nb, B_QBLK, HEAD_DIM).transpose(3, 0, 1, 2, 4, 5)
    scale = 1.0 / math.sqrt(HEAD_DIM)

    def q_block(qb):
        sc = jnp.einsum('bkgqd,bksd->bkgqs', qb, k).astype(jnp.float32) * scale
        pr = jax.nn.softmax(sc, axis=-1).astype(v.dtype)
        return jnp.einsum('bkgqs,bksd->bkgqd', pr, v)

    o = lax.map(q_block, q_blk)
    o = o.transpose(1, 0, 4, 2, 3, 5).reshape(b, s, MIX_WIDTH)
    return (o * jax.nn.silu(gate)) @ w_out


def setup_inputs(seed: int = 0) -> dict:
    key = jax.random.key(seed)
    ks = jax.random.split(key, 16)
    n_a = (DEPTH + NUM_MIXERS - 1) // NUM_MIXERS
    n_b = DEPTH // NUM_MIXERS
    kv_w = B_KV_HEADS * HEAD_DIM
    b_in_w = 2 * MIX_WIDTH + 2 * kv_w
    nrm = lambda k, shape, fan: jax.random.normal(k, shape, jnp.float32) * (fan ** -0.5)
    gain = lambda k, shape: 1.0 + 0.05 * jax.random.normal(k, shape, jnp.float32)
    return {
        "x": jax.random.normal(ks[0], (BATCH, SEQ, D_MODEL), jnp.float32),
        "p": jax.random.normal(ks[1], (DEPTH, BATCH, SEQ, PLE_DIM), jnp.float32),
        "norm_g": gain(ks[2], (DEPTH, D_MODEL)),
        "a_w_in": nrm(ks[3], (n_a, D_MODEL, 4 * MIX_WIDTH), D_MODEL),
        "a_rpb": 0.1 * jax.random.normal(ks[4], (n_a, A_HEADS, 2 * A_WIN_ROWS - 1, 2 * A_WIN_COLS - 1), jnp.float32),
        "a_w_out": nrm(ks[5], (n_a, MIX_WIDTH, D_MODEL), MIX_WIDTH),
        "b_w_in": nrm(ks[6], (n_b, D_MODEL, b_in_w), D_MODEL),
        "b_q_norm": gain(ks[7], (n_b, HEAD_DIM)),
        "b_k_norm": gain(ks[8], (n_b, HEAD_DIM)),
        "b_w_out": nrm(ks[9], (n_b, MIX_WIDTH, D_MODEL), MIX_WIDTH),
        "ple_norm_g": gain(ks[10], (DEPTH, D_MODEL)),
        "ple_w_gate": nrm(ks[11], (DEPTH, D_MODEL, D_MODEL), D_MODEL),
        "ple_w_proj": nrm(ks[12], (DEPTH, PLE_DIM, D_MODEL), PLE_DIM),
        "final_norm_g": gain(ks[13], (D_MODEL,)),
    }


def reference(x, p, norm_g, a_w_in, a_rpb, a_w_out, b_w_in, b_q_norm, b_k_norm,
              b_w_out, ple_norm_g, ple_w_gate, ple_w_proj, final_norm_g):
    for i in range(DEPTH):
        h = rms_norm(x, norm_g[i])
        j = i // NUM_MIXERS
        if i % NUM_MIXERS == 0:
            y = mixer_a(h, a_w_in[j], a_rpb[j], a_w_out[j])
        else:
            y = mixer_b(h, b_w_in[j], b_q_norm[j], b_k_norm[j], b_w_out[j])
        x = x + y
        g = jax.nn.sigmoid(rms_norm(x, ple_norm_g[i]) @ ple_w_gate[i])
        x = x + g * (p[i] @ ple_w_proj[i])
    return rms_norm(x, final_norm_g)
```

```python
import functools
import math

import jax
import jax.numpy as jnp
import numpy as np
from jax import lax
from jax.experimental import pallas as pl
from jax.experimental.pallas import tpu as pltpu

D_MODEL = 1024
SEQ = 4096
GRID_W = 64
GRID_ROWS = SEQ // GRID_W
HEAD_DIM = 64
HEADS = D_MODEL // HEAD_DIM
KV_HEADS_B = 4
KV_W_B = KV_HEADS_B * HEAD_DIM
WIN_ROWS = 8
WIN_COLS = 16
ROPE_THETA = 10000.0
ROPE_SECTION = HEAD_DIM // 2
PLE_DIM = 256
NORM_EPS = 1e-6
NUM_MIXERS = 2

LANES = 128
HEAD_PAIRS = HEADS // 2
NEG_BIAS = -1e30

ROW_TILE = 512
A_QROWS = 4
A_KROWS = 12
A_KCHUNK = 4
A_TQ = A_QROWS * GRID_W
A_TKC = A_KCHUNK * GRID_W
A_NKC = A_KROWS // A_KCHUNK
B_TQ = 256
VMEM_LIMIT = 56 * 1024 * 1024

BF16 = jnp.bfloat16
F32 = jnp.float32


def _rms(x, g):
    ms = jnp.mean(x * x, axis=-1, keepdims=True)
    return x * lax.rsqrt(ms + NORM_EPS) * g


def _dot(a, b):
    return jnp.dot(a, b, preferred_element_type=F32)


def _dot_nt(a, b):
    return lax.dot_general(a, b, (((1,), (1,)), ((), ())), preferred_element_type=F32)


def _lower_half_mask(shape):
    return lax.broadcasted_iota(jnp.int32, shape, len(shape) - 1) < HEAD_DIM


def _split_heads(q2):
    lo = _lower_half_mask((1, LANES)).astype(q2.dtype)
    return jnp.concatenate([q2 * lo, q2 * (1 - lo)], axis=0)


def _merge_heads(o2, n):
    return jnp.where(_lower_half_mask((n, LANES)), o2[:n], o2[n:])


def _silu(g):
    return g * jax.nn.sigmoid(g)


def _inproj_a_kernel(x_ref, g_ref, w_ref, q_ref, k_ref, v_ref, gate_ref):
    h = _rms(x_ref[...], g_ref[...]).astype(BF16)
    for c, o_ref in enumerate((q_ref, k_ref, v_ref, gate_ref)):
        o_ref[...] = _dot(h, w_ref[:, c * D_MODEL:(c + 1) * D_MODEL]).astype(BF16)


def _inproj_a(x, g, w):
    m = x.shape[0]
    out = jax.ShapeDtypeStruct((m, D_MODEL), BF16)
    row = lambda i: (i, 0)
    fixed = lambda i: (0, 0)
    return pl.pallas_call(
        _inproj_a_kernel,
        grid=(m // ROW_TILE,),
        in_specs=[pl.BlockSpec((ROW_TILE, D_MODEL), row),
                  pl.BlockSpec((1, D_MODEL), fixed),
                  pl.BlockSpec((D_MODEL, 4 * D_MODEL), fixed)],
        out_specs=[pl.BlockSpec((ROW_TILE, D_MODEL), row)] * 4,
        out_shape=[out] * 4,
        compiler_params=pltpu.CompilerParams(dimension_semantics=("parallel",), vmem_limit_bytes=VMEM_LIMIT),
        name="inproj_a",
    )(x, g, w)


def _inproj_b_kernel(x_ref, g_ref, w_ref, hg_ref, cos_ref, sin_ref, seg_ref, q_ref, k_ref, v_ref, gate_ref):
    qk_w = D_MODEL + KV_W_B
    h = _rms(x_ref[...], g_ref[...]).astype(BF16)
    seg = seg_ref[...]
    cos = cos_ref[...]
    sin = sin_ref[...]
    first_half = (lax.broadcasted_iota(jnp.int32, (1, LANES), 1) % ROPE_SECTION) < (ROPE_SECTION // 2)
    for c in range(qk_w // LANES):
        y = _dot(h, w_ref[:, c * LANES:(c + 1) * LANES])
        ss = y * y
        hi = ss.astype(BF16)
        lo = (ss - hi.astype(F32)).astype(BF16)
        ms = _dot(jnp.concatenate([hi, lo], axis=1), seg) * (1.0 / HEAD_DIM)
        yn = y * lax.rsqrt(ms + NORM_EPS) * hg_ref[:, c * LANES:(c + 1) * LANES]
        partner = jnp.where(first_half, pltpu.roll(yn, LANES - ROPE_SECTION // 2, 1),
                            pltpu.roll(yn, ROPE_SECTION // 2, 1))
        r = (yn * cos + partner * sin).astype(BF16)
        if c < D_MODEL // LANES:
            q_ref[:, c * LANES:(c + 1) * LANES] = r
        else:
            k_ref[:, (c - D_MODEL // LANES) * LANES:(c - D_MODEL // LANES + 1) * LANES] = r
    v_ref[...] = _dot(h, w_ref[:, qk_w:qk_w + KV_W_B]).astype(BF16)
    gate_ref[...] = _dot(h, w_ref[:, qk_w + KV_W_B:]).astype(BF16)


def _inproj_b(x, g, w, head_gain, cos, sin, seg):
    m = x.shape[0]
    n_w = 2 * D_MODEL + 2 * KV_W_B
    pos_blocks = SEQ // ROW_TILE
    row = lambda i: (i, 0)
    fixed = lambda i: (0, 0)
    pos = lambda i: (i % pos_blocks, 0)
    wide = jax.ShapeDtypeStruct((m, D_MODEL), BF16)
    narrow = jax.ShapeDtypeStruct((m, KV_W_B), BF16)
    return pl.pallas_call(
        _inproj_b_kernel,
        grid=(m // ROW_TILE,),
        in_specs=[pl.BlockSpec((ROW_TILE, D_MODEL), row),
                  pl.BlockSpec((1, D_MODEL), fixed),
                  pl.BlockSpec((D_MODEL, n_w), fixed),
                  pl.BlockSpec((1, D_MODEL + KV_W_B), fixed),
                  pl.BlockSpec((ROW_TILE, LANES), pos),
                  pl.BlockSpec((ROW_TILE, LANES), pos),
                  pl.BlockSpec((2 * LANES, LANES), fixed)],
        out_specs=[pl.BlockSpec((ROW_TILE, D_MODEL), row),
                   pl.BlockSpec((ROW_TILE, KV_W_B), row),
                   pl.BlockSpec((ROW_TILE, KV_W_B), row),
                   pl.BlockSpec((ROW_TILE, D_MODEL), row)],
        out_shape=[wide, narrow, narrow, wide],
        compiler_params=pltpu.CompilerParams(dimension_semantics=("parallel",), vmem_limit_bytes=VMEM_LIMIT),
        name="inproj_b",
    )(x, g, w, head_gain, cos, sin, seg)


def _outproj_kernel(a_ref, x_ref, p_ref, wo_ref, pg_ref, wg_ref, wp_ref, fg_ref, o_ref, *, final_norm):
    x1 = x_ref[...] + _dot(a_ref[...], wo_ref[...])
    hn = _rms(x1, pg_ref[...]).astype(BF16)
    gate = jax.nn.sigmoid(_dot(hn, wg_ref[...]))
    x2 = x1 + gate * _dot(p_ref[...].astype(BF16), wp_ref[...])
    o_ref[...] = _rms(x2, fg_ref[...]) if final_norm else x2


def _outproj(a, x, p_all, layer, wo, pg, wg, wp, fg, final_norm):
    m = x.shape[0]
    row = lambda i: (i, 0)
    fixed = lambda i: (0, 0)
    return pl.pallas_call(
        functools.partial(_outproj_kernel, final_norm=final_norm),
        grid=(m // ROW_TILE,),
        in_specs=[pl.BlockSpec((ROW_TILE, D_MODEL), row),
                  pl.BlockSpec((ROW_TILE, D_MODEL), row),
                  pl.BlockSpec((None, ROW_TILE, PLE_DIM), lambda i: (layer, i, 0)),
                  pl.BlockSpec((D_MODEL, D_MODEL), fixed),
                  pl.BlockSpec((1, D_MODEL), fixed),
                  pl.BlockSpec((D_MODEL, D_MODEL), fixed),
                  pl.BlockSpec((PLE_DIM, D_MODEL), fixed),
                  pl.BlockSpec((1, D_MODEL), fixed)],
        out_specs=pl.BlockSpec((ROW_TILE, D_MODEL), row),
        out_shape=jax.ShapeDtypeStruct((m, D_MODEL), F32),
        compiler_params=pltpu.CompilerParams(dimension_semantics=("parallel",), vmem_limit_bytes=VMEM_LIMIT),
        name="outproj_final" if final_norm else "outproj",
    )(a, x, p_all, wo, pg, wg, wp, fg)


def _natten_kernel(q_ref, k0_ref, k1_ref, k2_ref, v0_ref, v1_ref, v2_ref, gate_ref, bias_ref, o_ref):
    k_refs = (k0_ref, k1_ref, k2_ref)
    v_refs = (v0_ref, v1_ref, v2_ref)
    ones = jnp.ones((A_TKC, LANES), BF16)

    def one_batch(b, carry):
        qs = _split_heads(q_ref[b])
        s = [_dot_nt(qs, k_refs[c][b])
             + bias_ref[0, :, :, c * A_TKC:(c + 1) * A_TKC].reshape(2 * A_TQ, A_TKC)
             for c in range(A_NKC)]
        m = functools.reduce(jnp.maximum, [jnp.max(sc, axis=-1, keepdims=True) for sc in s])
        acc = None
        for c in range(A_NKC):
            p = jnp.exp(s[c] - m).astype(BF16)
            vext = jnp.concatenate([v_refs[c][b], ones], axis=1)
            pv = _dot(p, vext)
            acc = pv if acc is None else acc + pv
        o = _merge_heads(acc[:, :LANES] / acc[:, LANES:], A_TQ)
        o_ref[b] = (o * _silu(gate_ref[b].astype(F32))).astype(BF16)
        return carry

    lax.fori_loop(0, q_ref.shape[0], one_batch, 0)


def _natten(q, k, v, gate, bias):
    bsz = q.shape[0]
    n_blocks = GRID_ROWS // A_QROWS

    def kv_spec(c):
        return pl.BlockSpec((bsz, A_TKC, LANES),
                            lambda p, i: (0, jnp.clip(i - 1, 0, n_blocks - A_NKC) + c, p))

    def variant(i):
        return jnp.where(i == 0, 0, jnp.where(i == n_blocks - 1, 2, 1))

    q_spec = pl.BlockSpec((bsz, A_TQ, LANES), lambda p, i: (0, i, p))
    return pl.pallas_call(
        _natten_kernel,
        grid=(HEAD_PAIRS, n_blocks),
        in_specs=[q_spec, kv_spec(0), kv_spec(1), kv_spec(2), kv_spec(0), kv_spec(1), kv_spec(2), q_spec,
                  pl.BlockSpec((1, 2, A_TQ, A_KROWS * GRID_W), lambda p, i: (variant(i), p, 0, 0))],
        out_specs=q_spec,
        out_shape=jax.ShapeDtypeStruct(q.shape, BF16),
        compiler_params=pltpu.CompilerParams(dimension_semantics=("parallel", "parallel"),
                                             vmem_limit_bytes=VMEM_LIMIT),
        name="natten",
    )(q, k, k, k, v, v, v, gate, bias)


def _natten_bias(rpb):
    qc = np.arange(GRID_W)
    kc = np.arange(GRID_W)
    cs = np.clip(qc - WIN_COLS // 2, 0, GRID_W - WIN_COLS)
    col_ok = (kc[None, :] >= cs[:, None]) & (kc[None, :] < cs[:, None] + WIN_COLS)
    dc = np.clip(kc[None, :] - qc[:, None] + WIN_COLS - 1, 0, 2 * WIN_COLS - 2)
    table = jnp.where(col_ok[None, None], jnp.take(rpb, jnp.asarray(dc), axis=2), NEG_BIAS)
    masked = jnp.full((HEADS, GRID_W, GRID_W), NEG_BIAS, F32)
    n_blocks = GRID_ROWS // A_QROWS
    variants = []
    for i in (0, 1, n_blocks - 1):
        ws = A_KCHUNK * min(max(i - 1, 0), n_blocks - A_NKC)
        rows = []
        for a in range(A_QROWS):
            qr = A_QROWS * i + a
            rs = min(max(qr - WIN_ROWS // 2, 0), GRID_ROWS - WIN_ROWS)
            blocks = []
            for b in range(A_KROWS):
                kr = ws + b
                blocks.append(table[:, kr - qr + WIN_ROWS - 1] if rs <= kr < rs + WIN_ROWS else masked)
            rows.append(jnp.concatenate(blocks, axis=2))
        variants.append(jnp.concatenate(rows, axis=1))
    return jnp.stack(variants)


def _gqa_kernel(q_ref, k_ref, v_ref, gate_ref, o_ref, vext_ref):
    @pl.when((pl.program_id(2) == 0) & (pl.program_id(3) == 0))
    def _():
        vext_ref[:, :LANES] = v_ref[0]
        vext_ref[:, LANES:] = jnp.ones((SEQ, LANES), BF16)

    qs = _split_heads(q_ref[0])
    s = _dot_nt(qs, k_ref[0])
    m = jnp.max(s, axis=-1, keepdims=True)
    p = jnp.exp(s - m).astype(BF16)
    acc = _dot(p, vext_ref[...])
    o = _merge_heads(acc[:, :LANES] / acc[:, LANES:], B_TQ)
    o_ref[0] = (o * _silu(gate_ref[0].astype(F32))).astype(BF16)


def _gqa(q, k, v, gate):
    bsz = q.shape[0]
    blocks_per_pair = HEAD_PAIRS // (KV_HEADS_B // 2)
    q_spec = pl.BlockSpec((1, B_TQ, LANES), lambda b, j, i, u: (b, i, blocks_per_pair * j + u))
    kv_spec = pl.BlockSpec((1, SEQ, LANES), lambda b, j, i, u: (b, 0, j))
    return pl.pallas_call(
        _gqa_kernel,
        grid=(bsz, KV_HEADS_B // 2, SEQ // B_TQ, blocks_per_pair),
        in_specs=[q_spec, kv_spec, kv_spec, q_spec],
        out_specs=q_spec,
        out_shape=jax.ShapeDtypeStruct(q.shape, BF16),
        scratch_shapes=[pltpu.VMEM((SEQ, 2 * LANES), BF16)],
        compiler_params=pltpu.CompilerParams(
            dimension_semantics=("parallel", "parallel", "arbitrary", "arbitrary"),
            vmem_limit_bytes=VMEM_LIMIT),
        name="gqa",
    )(q, k, v, gate)


_B_HEAD_ORDER = [8 * (t // 4) + t % 4 + 4 * half for t in range(HEAD_PAIRS) for half in range(2)]


def _rope_tables():
    t = np.arange(SEQ)
    pos = np.stack([t // GRID_W, t % GRID_W], axis=1).astype(np.float32)
    lane = np.arange(LANES) % HEAD_DIM
    section = lane // ROPE_SECTION
    n_freq = ROPE_SECTION // 2
    inv = jnp.power(ROPE_THETA, -jnp.arange(n_freq, dtype=F32) * 2.0 / ROPE_SECTION)
    ang = jnp.asarray(pos)[:, section] * inv[lane % n_freq][None, :]
    sign = np.where(lane % ROPE_SECTION < n_freq, -1.0, 1.0).astype(np.float32)
    return jnp.cos(ang), jnp.sin(ang) * sign[None, :]


def kernel(x, p, norm_g, a_w_in, a_rpb, a_w_out, b_w_in, b_q_norm, b_k_norm, b_w_out, ple_norm_g, ple_w_gate,
           ple_w_proj, final_norm_g):
    bsz, seq, d = x.shape
    assert (seq, d) == (SEQ, D_MODEL)
    depth = p.shape[0]
    m = bsz * seq
    scale = 1.0 / math.sqrt(HEAD_DIM)
    xf = x.reshape(m, d)
    p_all = p.reshape(depth, m, PLE_DIM)
    cos, sin = _rope_tables()
    seg = np.kron(np.eye(2), np.ones((HEAD_DIM, HEAD_DIM)))
    seg = jnp.asarray(np.concatenate([seg, seg], axis=0), BF16)
    col_order = np.concatenate([np.arange(HEAD_DIM) + HEAD_DIM * h for h in _B_HEAD_ORDER])
    row2 = lambda v: v.reshape(1, -1)

    for i in range(depth):
        j = i // NUM_MIXERS
        if i % NUM_MIXERS == 0:
            w = a_w_in[j]
            w = jnp.concatenate([w[:, :D_MODEL] * scale, w[:, D_MODEL:]], axis=1).astype(BF16)
            q, k, v, gate = _inproj_a(xf, row2(norm_g[i]), w)
            to3 = lambda t: t.reshape(bsz, seq, D_MODEL)
            a = _natten(to3(q), to3(k), to3(v), to3(gate), _natten_bias(a_rpb[j]))
            wo = a_w_out[j].astype(BF16)
        else:
            w = b_w_in[j]
            qk_w = D_MODEL + KV_W_B
            w = jnp.concatenate([w[:, :D_MODEL][:, col_order], w[:, D_MODEL:qk_w + KV_W_B],
                                 w[:, qk_w + KV_W_B:][:, col_order]], axis=1).astype(BF16)
            head_gain = jnp.concatenate([jnp.tile(b_q_norm[j] * scale, HEADS), jnp.tile(b_k_norm[j], KV_HEADS_B)])
            q, k, v, gate = _inproj_b(xf, row2(norm_g[i]), w, row2(head_gain), cos, sin, seg)
            a = _gqa(q.reshape(bsz, seq, D_MODEL), k.reshape(bsz, seq, KV_W_B), v.reshape(bsz, seq, KV_W_B),
                     gate.reshape(bsz, seq, D_MODEL))
            wo = b_w_out[j][col_order, :].astype(BF16)
        xf = _outproj(a.reshape(m, D_MODEL), xf, p_all, i, wo, row2(ple_norm_g[i]), ple_w_gate[i].astype(BF16),
                      ple_w_proj[i].astype(BF16), row2(final_norm_g), final_norm=(i == depth - 1))
    return xf.reshape(bsz, seq, d)
```

```python
import functools
import math

import jax
import jax.numpy as jnp
import numpy as np
from jax import lax
from jax.experimental import pallas as pl
from jax.experimental.pallas import tpu as pltpu

D_MODEL = 1024
SEQ = 4096
GRID_W = 64
GRID_ROWS = SEQ // GRID_W
HEAD_DIM = 64
HEADS = D_MODEL // HEAD_DIM
KV_HEADS_B = 4
KV_W_B = KV_HEADS_B * HEAD_DIM
WIN_ROWS = 8
WIN_COLS = 16
ROPE_THETA = 10000.0
ROPE_SECTION = HEAD_DIM // 2
PLE_DIM = 256
NORM_EPS = 1e-6
NUM_MIXERS = 2

LANES = 128
HEAD_PAIRS = HEADS // 2
NEG_BIAS = -1e30

ROW_TILE = 512
A_QROWS = 4
A_KROWS = 12
A_KCHUNK = 4
A_TQ = A_QROWS * GRID_W
A_TKC = A_KCHUNK * GRID_W
A_NKC = A_KROWS // A_KCHUNK
B_TQ = 256
B_QGROUP = 1024
B_ONES_ROWS = 16
VMEM_LIMIT = 56 * 1024 * 1024

BF16 = jnp.bfloat16
F32 = jnp.float32


def _rms(x, g):
    ms = jnp.mean(x * x, axis=-1, keepdims=True)
    return x * lax.rsqrt(ms + NORM_EPS) * g


def _dot(a, b):
    return jnp.dot(a, b, preferred_element_type=F32)


def _dot_nt(a, b):
    return lax.dot_general(a, b, (((1,), (1,)), ((), ())), preferred_element_type=F32)


def _lower_half_mask(shape):
    return lax.broadcasted_iota(jnp.int32, shape, len(shape) - 1) < HEAD_DIM


def _split_heads(q2):
    lo = _lower_half_mask((1, LANES)).astype(q2.dtype)
    return jnp.concatenate([q2 * lo, q2 * (1 - lo)], axis=0)


def _merge_heads(o2, n):
    return jnp.where(_lower_half_mask((n, LANES)), o2[:n], o2[n:])


def _silu(g):
    return g * jax.nn.sigmoid(g)


def _inproj_a_kernel(x_ref, g_ref, w_ref, q_ref, k_ref, v_ref, gate_ref):
    h = _rms(x_ref[...], g_ref[...]).astype(BF16)
    for c, o_ref in enumerate((q_ref, k_ref, v_ref, gate_ref)):
        o_ref[...] = _dot(h, w_ref[:, c * D_MODEL:(c + 1) * D_MODEL]).astype(BF16)


def _inproj_a(x, g, w):
    m = x.shape[0]
    out = jax.ShapeDtypeStruct((m, D_MODEL), BF16)
    row = lambda i: (i, 0)
    fixed = lambda i: (0, 0)
    return pl.pallas_call(
        _inproj_a_kernel,
        grid=(m // ROW_TILE,),
        in_specs=[pl.BlockSpec((ROW_TILE, D_MODEL), row),
                  pl.BlockSpec((1, D_MODEL), fixed),
                  pl.BlockSpec((D_MODEL, 4 * D_MODEL), fixed)],
        out_specs=[pl.BlockSpec((ROW_TILE, D_MODEL), row)] * 4,
        out_shape=[out] * 4,
        compiler_params=pltpu.CompilerParams(dimension_semantics=("parallel",), vmem_limit_bytes=VMEM_LIMIT),
        name="inproj_a",
    )(x, g, w)


def _inproj_b_kernel(x_ref, g_ref, w_ref, hg_ref, cos_ref, sin_ref, seg_ref, q_ref, k_ref, v_ref, gate_ref):
    qk_w = D_MODEL + KV_W_B
    h = _rms(x_ref[...], g_ref[...]).astype(BF16)
    seg = seg_ref[...]
    cos = cos_ref[...]
    sin = sin_ref[...]
    first_half = (lax.broadcasted_iota(jnp.int32, (1, LANES), 1) % ROPE_SECTION) < (ROPE_SECTION // 2)
    for c in range(qk_w // LANES):
        y = _dot(h, w_ref[:, c * LANES:(c + 1) * LANES])
        ss = y * y
        hi = ss.astype(BF16)
        lo = (ss - hi.astype(F32)).astype(BF16)
        ms = _dot(jnp.concatenate([hi, lo], axis=1), seg) * (1.0 / HEAD_DIM)
        yn = y * lax.rsqrt(ms + NORM_EPS) * hg_ref[:, c * LANES:(c + 1) * LANES]
        partner = jnp.where(first_half, pltpu.roll(yn, LANES - ROPE_SECTION // 2, 1),
                            pltpu.roll(yn, ROPE_SECTION // 2, 1))
        r = (yn * cos + partner * sin).astype(BF16)
        if c < D_MODEL // LANES:
            q_ref[c] = r
        else:
            k_ref[:, (c - D_MODEL // LANES) * LANES:(c - D_MODEL // LANES + 1) * LANES] = r
    v_ref[...] = _dot(h, w_ref[:, qk_w:qk_w + KV_W_B]).astype(BF16)
    for c in range(D_MODEL // LANES):
        col = qk_w + KV_W_B + c * LANES
        gate_ref[c] = _dot(h, w_ref[:, col:col + LANES]).astype(BF16)


def _inproj_b(x, g, w, head_gain, cos, sin, seg):
    m = x.shape[0]
    n_w = 2 * D_MODEL + 2 * KV_W_B
    pos_blocks = SEQ // ROW_TILE
    row = lambda i: (i, 0)
    fixed = lambda i: (0, 0)
    pos = lambda i: (i % pos_blocks, 0)
    wide = jax.ShapeDtypeStruct((D_MODEL // LANES, m, LANES), BF16)
    wide_spec = pl.BlockSpec((D_MODEL // LANES, ROW_TILE, LANES), lambda i: (0, i, 0))
    narrow = jax.ShapeDtypeStruct((m, KV_W_B), BF16)
    return pl.pallas_call(
        _inproj_b_kernel,
        grid=(m // ROW_TILE,),
        in_specs=[pl.BlockSpec((ROW_TILE, D_MODEL), row),
                  pl.BlockSpec((1, D_MODEL), fixed),
                  pl.BlockSpec((D_MODEL, n_w), fixed),
                  pl.BlockSpec((1, D_MODEL + KV_W_B), fixed),
                  pl.BlockSpec((ROW_TILE, LANES), pos),
                  pl.BlockSpec((ROW_TILE, LANES), pos),
                  pl.BlockSpec((2 * LANES, LANES), fixed)],
        out_specs=[wide_spec,
                   pl.BlockSpec((ROW_TILE, KV_W_B), row),
                   pl.BlockSpec((ROW_TILE, KV_W_B), row),
                   wide_spec],
        out_shape=[wide, narrow, narrow, wide],
        compiler_params=pltpu.CompilerParams(dimension_semantics=("parallel",), vmem_limit_bytes=VMEM_LIMIT),
        name="inproj_b",
    )(x, g, w, head_gain, cos, sin, seg)


def _outproj_kernel(a_ref, x_ref, p_ref, wo_ref, pg_ref, wg_ref, wp_ref, fg_ref, o_ref, *, final_norm):
    if len(a_ref.shape) == 3:
        a = jnp.concatenate([a_ref[c] for c in range(a_ref.shape[0])], axis=1)
    else:
        a = a_ref[...]
    x1 = x_ref[...] + _dot(a, wo_ref[...])
    hn = _rms(x1, pg_ref[...]).astype(BF16)
    gate = jax.nn.sigmoid(_dot(hn, wg_ref[...]))
    x2 = x1 + gate * _dot(p_ref[...].astype(BF16), wp_ref[...])
    o_ref[...] = _rms(x2, fg_ref[...]) if final_norm else x2


def _outproj(a, x, p_all, layer, wo, pg, wg, wp, fg, final_norm):
    m = x.shape[0]
    row = lambda i: (i, 0)
    fixed = lambda i: (0, 0)
    return pl.pallas_call(
        functools.partial(_outproj_kernel, final_norm=final_norm),
        grid=(m // ROW_TILE,),
        in_specs=[pl.BlockSpec((ROW_TILE, D_MODEL), row) if a.ndim == 2 else
                  pl.BlockSpec((a.shape[0], ROW_TILE, LANES), lambda i: (0, i, 0)),
                  pl.BlockSpec((ROW_TILE, D_MODEL), row),
                  pl.BlockSpec((None, ROW_TILE, PLE_DIM), lambda i: (layer, i, 0)),
                  pl.BlockSpec((D_MODEL, D_MODEL), fixed),
                  pl.BlockSpec((1, D_MODEL), fixed),
                  pl.BlockSpec((D_MODEL, D_MODEL), fixed),
                  pl.BlockSpec((PLE_DIM, D_MODEL), fixed),
                  pl.BlockSpec((1, D_MODEL), fixed)],
        out_specs=pl.BlockSpec((ROW_TILE, D_MODEL), row),
        out_shape=jax.ShapeDtypeStruct((m, D_MODEL), F32),
        compiler_params=pltpu.CompilerParams(dimension_semantics=("parallel",), vmem_limit_bytes=VMEM_LIMIT),
        name="outproj_final" if final_norm else "outproj",
    )(a, x, p_all, wo, pg, wg, wp, fg)


def _natten_kernel(q_ref, k0_ref, k1_ref, k2_ref, v0_ref, v1_ref, v2_ref, gate_ref, bias_ref, o_ref):
    k_refs = (k0_ref, k1_ref, k2_ref)
    v_refs = (v0_ref, v1_ref, v2_ref)
    ones = jnp.ones((A_TKC, LANES), BF16)

    def one_batch(b, carry):
        qs = _split_heads(q_ref[b])
        s = [_dot_nt(qs, k_refs[c][b])
             + bias_ref[0, :, :, c * A_TKC:(c + 1) * A_TKC].reshape(2 * A_TQ, A_TKC)
             for c in range(A_NKC)]
        m = functools.reduce(jnp.maximum, [jnp.max(sc, axis=-1, keepdims=True) for sc in s])
        acc = None
        for c in range(A_NKC):
            p = jnp.exp(s[c] - m).astype(BF16)
            vext = jnp.concatenate([v_refs[c][b], ones], axis=1)
            pv = _dot(p, vext)
            acc = pv if acc is None else acc + pv
        o = _merge_heads(acc[:, :LANES] / acc[:, LANES:], A_TQ)
        o_ref[b] = (o * _silu(gate_ref[b].astype(F32))).astype(BF16)
        return carry

    lax.fori_loop(0, q_ref.shape[0], one_batch, 0)


def _natten(q, k, v, gate, bias):
    bsz = q.shape[0]
    n_blocks = GRID_ROWS // A_QROWS

    def kv_spec(c):
        return pl.BlockSpec((bsz, A_TKC, LANES),
                            lambda p, i: (0, jnp.clip(i - 1, 0, n_blocks - A_NKC) + c, p))

    def variant(i):
        return jnp.where(i == 0, 0, jnp.where(i == n_blocks - 1, 2, 1))

    q_spec = pl.BlockSpec((bsz, A_TQ, LANES), lambda p, i: (0, i, p))
    return pl.pallas_call(
        _natten_kernel,
        grid=(HEAD_PAIRS, n_blocks),
        in_specs=[q_spec, kv_spec(0), kv_spec(1), kv_spec(2), kv_spec(0), kv_spec(1), kv_spec(2), q_spec,
                  pl.BlockSpec((1, 2, A_TQ, A_KROWS * GRID_W), lambda p, i: (variant(i), p, 0, 0))],
        out_specs=q_spec,
        out_shape=jax.ShapeDtypeStruct(q.shape, BF16),
        compiler_params=pltpu.CompilerParams(dimension_semantics=("parallel", "parallel"),
                                             vmem_limit_bytes=VMEM_LIMIT),
        name="natten",
    )(q, k, k, k, v, v, v, gate, bias)


def _natten_bias(rpb):
    qc = np.arange(GRID_W)
    kc = np.arange(GRID_W)
    cs = np.clip(qc - WIN_COLS // 2, 0, GRID_W - WIN_COLS)
    col_ok = (kc[None, :] >= cs[:, None]) & (kc[None, :] < cs[:, None] + WIN_COLS)
    dc = np.clip(kc[None, :] - qc[:, None] + WIN_COLS - 1, 0, 2 * WIN_COLS - 2)
    table = jnp.where(col_ok[None, None], jnp.take(rpb, jnp.asarray(dc), axis=2), NEG_BIAS)
    masked = jnp.full((HEADS, GRID_W, GRID_W), NEG_BIAS, F32)
    n_blocks = GRID_ROWS // A_QROWS
    variants = []
    for i in (0, 1, n_blocks - 1):
        ws = A_KCHUNK * min(max(i - 1, 0), n_blocks - A_NKC)
        rows = []
        for a in range(A_QROWS):
            qr = A_QROWS * i + a
            rs = min(max(qr - WIN_ROWS // 2, 0), GRID_ROWS - WIN_ROWS)
            blocks = []
            for b in range(A_KROWS):
                kr = ws + b
                blocks.append(table[:, kr - qr + WIN_ROWS - 1] if rs <= kr < rs + WIN_ROWS else masked)
            rows.append(jnp.concatenate(blocks, axis=2))
        variants.append(jnp.concatenate(rows, axis=1))
    return jnp.stack(variants)


def _gqa_kernel(q_ref, k_ref, v_ref, gate_ref, o_ref, vt_ref, st_ref):
    blocks, qgroup, _ = q_ref.shape
    units = blocks * (qgroup // B_TQ)

    @pl.when(pl.program_id(2) == 0)
    def _():
        vt = v_ref[0].astype(F32).T
        ones = jnp.ones((B_ONES_ROWS, SEQ), BF16)
        for hh in range(2):
            vt_ref[hh] = jnp.concatenate([vt[hh * HEAD_DIM:(hh + 1) * HEAD_DIM].astype(BF16), ones], axis=0)

    lo = _lower_half_mask((1, LANES)).astype(BF16)
    masks = (lo, 1 - lo)

    def rows(n):
        start = (n // blocks) * B_TQ
        return pl.ds(start if isinstance(n, int) else pl.multiple_of(start, B_TQ), B_TQ)

    def scores(n, slot):
        q2 = q_ref[n % blocks, rows(n), :]
        for hh in range(2):
            st_ref[slot, hh] = _dot_nt(k_ref[0], q2 * masks[hh])

    def finish(n, slot):
        halves = []
        for hh in range(2):
            st = st_ref[slot, hh]
            m = jnp.max(st, axis=0, keepdims=True)
            pt = jnp.exp(st - m).astype(BF16)
            acc = _dot(vt_ref[hh], pt)
            halves.append(acc[:HEAD_DIM] / acc[HEAD_DIM:HEAD_DIM + 1])
        o = jnp.concatenate(halves, axis=0).T
        g = gate_ref[n % blocks, rows(n), :].astype(F32)
        o_ref[n % blocks, rows(n), :] = (o * _silu(g)).astype(BF16)

    scores(0, 0)

    def body(i, carry):
        scores(2 * i + 1, 1)
        finish(2 * i, 0)
        scores(2 * i + 2, 0)
        finish(2 * i + 1, 1)
        return carry

    lax.fori_loop(0, units // 2 - 1, body, 0)
    scores(units - 1, 1)
    finish(units - 2, 0)
    finish(units - 1, 1)


def _gqa(q, k, v, gate):
    bsz = k.shape[0]
    blocks_per_pair = HEAD_PAIRS // (KV_HEADS_B // 2)
    groups = SEQ // B_QGROUP
    q_spec = pl.BlockSpec((blocks_per_pair, B_QGROUP, LANES), lambda b, j, g: (j, b * groups + g, 0))
    kv_spec = pl.BlockSpec((1, SEQ, LANES), lambda b, j, g: (b, 0, j))
    return pl.pallas_call(
        _gqa_kernel,
        grid=(bsz, KV_HEADS_B // 2, groups),
        in_specs=[q_spec, kv_spec, kv_spec, q_spec],
        out_specs=q_spec,
        out_shape=jax.ShapeDtypeStruct(q.shape, BF16),
        scratch_shapes=[pltpu.VMEM((2, HEAD_DIM + B_ONES_ROWS, SEQ), BF16),
                        pltpu.VMEM((2, 2, SEQ, B_TQ), F32)],
        compiler_params=pltpu.CompilerParams(
            dimension_semantics=("parallel", "parallel", "arbitrary"),
            vmem_limit_bytes=VMEM_LIMIT),
        name="gqa",
    )(q, k, v, gate)


_B_HEAD_ORDER = [8 * (t // 4) + t % 4 + 4 * half for t in range(HEAD_PAIRS) for half in range(2)]


def _rope_tables():
    t = np.arange(SEQ)
    pos = np.stack([t // GRID_W, t % GRID_W], axis=1).astype(np.float32)
    lane = np.arange(LANES) % HEAD_DIM
    section = lane // ROPE_SECTION
    n_freq = ROPE_SECTION // 2
    inv = jnp.power(ROPE_THETA, -jnp.arange(n_freq, dtype=F32) * 2.0 / ROPE_SECTION)
    ang = jnp.asarray(pos)[:, section] * inv[lane % n_freq][None, :]
    sign = np.where(lane % ROPE_SECTION < n_freq, -1.0, 1.0).astype(np.float32)
    return jnp.cos(ang), jnp.sin(ang) * sign[None, :]


def kernel(x, p, norm_g, a_w_in, a_rpb, a_w_out, b_w_in, b_q_norm, b_k_norm, b_w_out, ple_norm_g, ple_w_gate,
           ple_w_proj, final_norm_g):
    bsz, seq, d = x.shape
    assert (seq, d) == (SEQ, D_MODEL)
    depth = p.shape[0]
    m = bsz * seq
    scale = 1.0 / math.sqrt(HEAD_DIM)
    xf = x.reshape(m, d)
    p_all = p.reshape(depth, m, PLE_DIM)
    cos, sin = _rope_tables()
    seg = np.kron(np.eye(2), np.ones((HEAD_DIM, HEAD_DIM)))
    seg = jnp.asarray(np.concatenate([seg, seg], axis=0), BF16)
    col_order = np.concatenate([np.arange(HEAD_DIM) + HEAD_DIM * h for h in _B_HEAD_ORDER])
    row2 = lambda v: v.reshape(1, -1)

    for i in range(depth):
        j = i // NUM_MIXERS
        if i % NUM_MIXERS == 0:
            w = a_w_in[j]
            w = jnp.concatenate([w[:, :D_MODEL] * scale, w[:, D_MODEL:]], axis=1).astype(BF16)
            q, k, v, gate = _inproj_a(xf, row2(norm_g[i]), w)
            to3 = lambda t: t.reshape(bsz, seq, D_MODEL)
            a = _natten(to3(q), to3(k), to3(v), to3(gate), _natten_bias(a_rpb[j])).reshape(m, D_MODEL)
            wo = a_w_out[j].astype(BF16)
        else:
            w = b_w_in[j]
            qk_w = D_MODEL + KV_W_B
            w = jnp.concatenate([w[:, :D_MODEL][:, col_order], w[:, D_MODEL:qk_w + KV_W_B],
                                 w[:, qk_w + KV_W_B:][:, col_order]], axis=1).astype(BF16)
            head_gain = jnp.concatenate([jnp.tile(b_q_norm[j] * scale, HEADS), jnp.tile(b_k_norm[j], KV_HEADS_B)])
            q, k, v, gate = _inproj_b(xf, row2(norm_g[i]), w, row2(head_gain), cos, sin, seg)
            a = _gqa(q, k.reshape(bsz, seq, KV_W_B), v.reshape(bsz, seq, KV_W_B), gate)
            wo = b_w_out[j][col_order, :].astype(BF16)
        xf = _outproj(a, xf, p_all, i, wo, row2(ple_norm_g[i]), ple_w_gate[i].astype(BF16),
                      ple_w_proj[i].astype(BF16), row2(final_norm_g), final_norm=(i == depth - 1))
    return xf.reshape(bsz, seq, d)
```

```python
import functools
import math

import jax
import jax.numpy as jnp
import numpy as np
from jax import lax
from jax.experimental import pallas as pl
from jax.experimental.pallas import tpu as pltpu

D_MODEL = 1024
SEQ = 4096
GRID_W = 64
GRID_ROWS = SEQ // GRID_W
HEAD_DIM = 64
HEADS = D_MODEL // HEAD_DIM
KV_HEADS_B = 4
KV_W_B = KV_HEADS_B * HEAD_DIM
WIN_ROWS = 8
WIN_COLS = 16
ROPE_THETA = 10000.0
ROPE_SECTION = HEAD_DIM // 2
PLE_DIM = 256
NORM_EPS = 1e-6
NUM_MIXERS = 2

LANES = 128
HEAD_PAIRS = HEADS // 2
NEG_BIAS = -1e30
ONES_ROWS = 16

ROW_TILE = 512
A_QROWS = 4
A_KROWS = 12
A_KCHUNK = 4
A_TQ = A_QROWS * GRID_W
A_TK = A_KROWS * GRID_W
A_TKC = A_KCHUNK * GRID_W
A_NKC = A_KROWS // A_KCHUNK
A_BATCH = 2
A_SLOTS, A_AHEAD = 4, 2
B_SLOTS, B_AHEAD = 2, 1
B_TQ = 256
B_QGROUP = 2048
B_TKC = 512
VMEM_LIMIT = 56 * 1024 * 1024

BF16 = jnp.bfloat16
F32 = jnp.float32


def _rms(x, g):
    ms = jnp.mean(x * x, axis=-1, keepdims=True)
    return x * lax.rsqrt(ms + NORM_EPS) * g


def _dot(a, b):
    return jnp.dot(a, b, preferred_element_type=F32)


def _dot_nt(a, b):
    return lax.dot_general(a, b, (((1,), (1,)), ((), ())), preferred_element_type=F32)


def _head_masks():
    lo = (lax.broadcasted_iota(jnp.int32, (1, LANES), 1) < HEAD_DIM).astype(BF16)
    return lo, 1 - lo


def _silu(g):
    return g * jax.nn.sigmoid(g)


def _score_steps(st_ref, m_ref, slot, key_chunk, n_chunks, chunk_scores):
    maxes = ([], [])

    def chunk(c):
        keys = pl.ds(c * key_chunk, key_chunk)
        for hh in range(2):
            st = chunk_scores(hh, keys)
            st_ref[slot, hh, keys, :] = st
            maxes[hh].append(jnp.max(st, axis=0, keepdims=True))

    def done():
        for hh in range(2):
            m_ref[slot, hh] = functools.reduce(jnp.maximum, maxes[hh])

    return [functools.partial(chunk, c) for c in range(n_chunks)] + [done]


def _softmax_pv_steps(st_ref, m_ref, slot, key_chunk, n_chunks, vt_chunk, emit):
    accs = [None, None]

    def chunk(c):
        keys = pl.ds(c * key_chunk, key_chunk)
        for hh in range(2):
            pt = jnp.exp2(st_ref[slot, hh, keys, :] - m_ref[slot, hh]).astype(BF16)
            pv = _dot(vt_chunk(hh, c), pt)
            accs[hh] = pv if accs[hh] is None else accs[hh] + pv

    def done():
        emit(jnp.concatenate([a[:HEAD_DIM] / a[HEAD_DIM:HEAD_DIM + 1] for a in accs], axis=0))

    return [functools.partial(chunk, c) for c in range(n_chunks)] + [done]


def _pipeline(units, slots, ahead, scores, finish):
    assert units % slots == 0 and 0 < ahead < slots

    def run(*step_lists):
        for steps in zip(*step_lists):
            for step in reversed(steps):
                step()

    def trip(base, last):
        for k in range(slots):
            if not last or k + ahead < slots:
                run(scores(base + k + ahead, (k + ahead) % slots), finish(base + k, k))
            else:
                run(finish(base + k, k))

    for n in range(ahead):
        run(scores(n, n))

    def body(t, carry):
        trip(t * slots, False)
        return carry

    lax.fori_loop(0, units // slots - 1, body, 0)
    trip(units - slots, True)


def _inproj_a_kernel(x_ref, g_ref, w_ref, wvt_ref, q_ref, k_ref, vt_ref, gate_ref):
    h = _rms(x_ref[...], g_ref[...]).astype(BF16)
    for c, o_ref in enumerate((q_ref, k_ref, gate_ref)):
        o_ref[...] = _dot(h, w_ref[:, c * D_MODEL:(c + 1) * D_MODEL]).astype(BF16)
    vt = _dot_nt(wvt_ref[...], h).astype(BF16)
    for c in range(vt_ref.shape[1]):
        vt_ref[0, c] = vt[:, c * A_TKC:(c + 1) * A_TKC]


def _inproj_a(x, g, w, wvt):
    m = x.shape[0]
    out = jax.ShapeDtypeStruct((m, D_MODEL), BF16)
    tiles = SEQ // ROW_TILE
    fixed = lambda i: (0, 0)
    row_spec = pl.BlockSpec((ROW_TILE, D_MODEL), lambda i: (i, 0))
    return pl.pallas_call(
        _inproj_a_kernel,
        grid=(m // ROW_TILE,),
        in_specs=[row_spec,
                  pl.BlockSpec((1, D_MODEL), fixed),
                  pl.BlockSpec((D_MODEL, 3 * D_MODEL), fixed),
                  pl.BlockSpec((D_MODEL, D_MODEL), fixed)],
        out_specs=[row_spec, row_spec,
                   pl.BlockSpec((1, ROW_TILE // A_TKC, D_MODEL, A_TKC), lambda i: (i // tiles, i % tiles, 0, 0)),
                   row_spec],
        out_shape=[out, out, jax.ShapeDtypeStruct((m // SEQ, SEQ // A_TKC, D_MODEL, A_TKC), BF16), out],
        compiler_params=pltpu.CompilerParams(dimension_semantics=("parallel",), vmem_limit_bytes=VMEM_LIMIT),
        name="inproj_a",
    )(x, g, w, wvt)


def _inproj_b_kernel(x_ref, g_ref, w_ref, wvt_ref, hg_ref, cos_ref, sin_ref, seg_ref, q_ref, k_ref, vt_ref,
                     gate_ref):
    qk_w = D_MODEL + KV_W_B
    h = _rms(x_ref[...], g_ref[...]).astype(BF16)
    seg = seg_ref[...]
    cos = cos_ref[...]
    sin = sin_ref[...]
    first_half = (lax.broadcasted_iota(jnp.int32, (1, LANES), 1) % ROPE_SECTION) < (ROPE_SECTION // 2)
    yqk = _dot(h, w_ref[:, :qk_w])
    wide = 2 * LANES
    for pb in range(qk_w // wide):
        y = yqk[:, pb * wide:(pb + 1) * wide]
        ss = y * y
        hi = ss.astype(BF16)
        lo = (ss - hi.astype(F32)).astype(BF16)
        ms = _dot(jnp.concatenate([hi, lo], axis=1), seg) * (1.0 / HEAD_DIM)
        yn2 = y * lax.rsqrt(ms + NORM_EPS) * hg_ref[:, pb * wide:(pb + 1) * wide]
        for half in range(2):
            c = 2 * pb + half
            yn = yn2[:, half * LANES:(half + 1) * LANES]
            partner = jnp.where(first_half, pltpu.roll(yn, LANES - ROPE_SECTION // 2, 1),
                                pltpu.roll(yn, ROPE_SECTION // 2, 1))
            r = (yn * cos + partner * sin).astype(BF16)
            if c < D_MODEL // LANES:
                q_ref[c] = r
            else:
                k_ref[:, (c - D_MODEL // LANES) * LANES:(c - D_MODEL // LANES + 1) * LANES] = r
    vt_ref[0] = _dot_nt(wvt_ref[...], h).astype(BF16)
    gate = _dot(h, w_ref[:, qk_w:])
    for c in range(D_MODEL // LANES):
        gate_ref[c] = gate[:, c * LANES:(c + 1) * LANES].astype(BF16)


def _inproj_b(x, g, w, wvt, head_gain, cos, sin, seg):
    m = x.shape[0]
    n_w = 2 * D_MODEL + KV_W_B
    tiles = SEQ // ROW_TILE
    row = lambda i: (i, 0)
    fixed = lambda i: (0, 0)
    pos = lambda i: (i % tiles, 0)
    wide = jax.ShapeDtypeStruct((D_MODEL // LANES, m, LANES), BF16)
    wide_spec = pl.BlockSpec((D_MODEL // LANES, ROW_TILE, LANES), lambda i: (0, i, 0))
    return pl.pallas_call(
        _inproj_b_kernel,
        grid=(m // ROW_TILE,),
        in_specs=[pl.BlockSpec((ROW_TILE, D_MODEL), row),
                  pl.BlockSpec((1, D_MODEL), fixed),
                  pl.BlockSpec((D_MODEL, n_w), fixed),
                  pl.BlockSpec((KV_W_B, D_MODEL), fixed),
                  pl.BlockSpec((1, D_MODEL + KV_W_B), fixed),
                  pl.BlockSpec((ROW_TILE, LANES), pos),
                  pl.BlockSpec((ROW_TILE, LANES), pos),
                  pl.BlockSpec((4 * LANES, 2 * LANES), fixed)],
        out_specs=[wide_spec,
                   pl.BlockSpec((ROW_TILE, KV_W_B), row),
                   pl.BlockSpec((1, KV_W_B, ROW_TILE), lambda i: (i // tiles, 0, i % tiles)),
                   wide_spec],
        out_shape=[wide, jax.ShapeDtypeStruct((m, KV_W_B), BF16),
                   jax.ShapeDtypeStruct((m // SEQ, KV_W_B, SEQ), BF16), wide],
        compiler_params=pltpu.CompilerParams(dimension_semantics=("parallel",), vmem_limit_bytes=VMEM_LIMIT),
        name="inproj_b",
    )(x, g, w, wvt, head_gain, cos, sin, seg)


def _outproj_kernel(a_ref, x_ref, p_ref, wo_ref, pg_ref, wg_ref, wp_ref, fg_ref, o_ref, *, final_norm):
    if len(a_ref.shape) == 3:
        a = jnp.concatenate([a_ref[c] for c in range(a_ref.shape[0])], axis=1)
    else:
        a = a_ref[...]
    x1 = x_ref[...] + _dot(a, wo_ref[...])
    hn = _rms(x1, pg_ref[...]).astype(BF16)
    gate = jax.nn.sigmoid(_dot(hn, wg_ref[...]))
    x2 = x1 + gate * _dot(p_ref[...].astype(BF16), wp_ref[...])
    o_ref[...] = _rms(x2, fg_ref[...]) if final_norm else x2


def _outproj(a, x, p_all, layer, wo, pg, wg, wp, fg, final_norm):
    m = x.shape[0]
    row = lambda i: (i, 0)
    fixed = lambda i: (0, 0)
    return pl.pallas_call(
        functools.partial(_outproj_kernel, final_norm=final_norm),
        grid=(m // ROW_TILE,),
        in_specs=[pl.BlockSpec((ROW_TILE, D_MODEL), row) if a.ndim == 2 else
                  pl.BlockSpec((a.shape[0], ROW_TILE, LANES), lambda i: (0, i, 0)),
                  pl.BlockSpec((ROW_TILE, D_MODEL), row),
                  pl.BlockSpec((None, ROW_TILE, PLE_DIM), lambda i: (layer, i, 0)),
                  pl.BlockSpec((D_MODEL, D_MODEL), fixed),
                  pl.BlockSpec((1, D_MODEL), fixed),
                  pl.BlockSpec((D_MODEL, D_MODEL), fixed),
                  pl.BlockSpec((PLE_DIM, D_MODEL), fixed),
                  pl.BlockSpec((1, D_MODEL), fixed)],
        out_specs=pl.BlockSpec((ROW_TILE, D_MODEL), row),
        out_shape=jax.ShapeDtypeStruct((m, D_MODEL), F32),
        compiler_params=pltpu.CompilerParams(dimension_semantics=("parallel",), vmem_limit_bytes=VMEM_LIMIT),
        name="outproj_final" if final_norm else "outproj",
    )(a, x, p_all, wo, pg, wg, wp, fg)


def _natten_kernel(q_ref, k_ref, vt_ref, gate_ref, bias_ref, o_ref, st_ref, m_ref):
    n_blocks = GRID_ROWS // A_QROWS
    masks = _head_masks()
    ones = jnp.ones((ONES_ROWS, A_TKC), BF16)

    def where(n):
        n = jnp.asarray(n, jnp.int32)
        bb, i = n // n_blocks, n % n_blocks
        first_chunk = jnp.clip(i - 1, 0, n_blocks - A_NKC)
        variant = jnp.where(i == 0, 0, jnp.where(i == n_blocks - 1, 2, 1))
        return bb, pl.ds(pl.multiple_of(i * A_TQ, A_TQ), A_TQ), first_chunk, variant

    def scores(n, slot):
        bb, q_rows, first_chunk, variant = where(n)
        q2 = q_ref[bb, q_rows, :]
        qm = [q2 * masks[hh] for hh in range(2)]

        def chunk_scores(hh, keys):
            k_rows = pl.ds(pl.multiple_of(first_chunk * A_TKC + keys.start, A_TKC), A_TKC)
            return _dot_nt(k_ref[bb, k_rows, :], qm[hh]) + bias_ref[variant, hh, keys, :]

        return _score_steps(st_ref, m_ref, slot, A_TKC, A_NKC, chunk_scores)

    def finish(n, slot):
        bb, q_rows, first_chunk, _ = where(n)

        def vt_chunk(hh, c):
            return jnp.concatenate([vt_ref[bb, first_chunk + c, hh * HEAD_DIM:(hh + 1) * HEAD_DIM, :], ones], axis=0)

        def emit(ot):
            o_ref[bb, q_rows, :] = (ot.T * _silu(gate_ref[bb, q_rows, :].astype(F32))).astype(BF16)

        return _softmax_pv_steps(st_ref, m_ref, slot, A_TKC, A_NKC, vt_chunk, emit)

    _pipeline(q_ref.shape[0] * n_blocks, st_ref.shape[0], A_AHEAD, scores, finish)


def _natten(q, k, vt, gate, bias):
    bsz = q.shape[0]
    tok_spec = pl.BlockSpec((A_BATCH, SEQ, LANES), lambda p, g: (g, 0, p))
    return pl.pallas_call(
        _natten_kernel,
        grid=(HEAD_PAIRS, bsz // A_BATCH),
        in_specs=[tok_spec, tok_spec,
                  pl.BlockSpec((A_BATCH, SEQ // A_TKC, LANES, A_TKC), lambda p, g: (g, 0, p, 0)),
                  tok_spec,
                  pl.BlockSpec((3, 2, A_TK, A_TQ), lambda p, g: (0, p, 0, 0))],
        out_specs=tok_spec,
        out_shape=jax.ShapeDtypeStruct(q.shape, BF16),
        scratch_shapes=[pltpu.VMEM((A_SLOTS, 2, A_TK, A_TQ), F32), pltpu.VMEM((A_SLOTS, 2, 1, A_TQ), F32)],
        compiler_params=pltpu.CompilerParams(dimension_semantics=("parallel", "parallel"),
                                             vmem_limit_bytes=VMEM_LIMIT),
        name="natten",
    )(q, k, vt, gate, bias)


def _natten_bias(rpb):
    qc = np.arange(GRID_W)
    kc = np.arange(GRID_W)
    cs = np.clip(qc - WIN_COLS // 2, 0, GRID_W - WIN_COLS)
    col_ok = (kc[:, None] >= cs[None, :]) & (kc[:, None] < cs[None, :] + WIN_COLS)
    dc = np.clip(kc[:, None] - qc[None, :] + WIN_COLS - 1, 0, 2 * WIN_COLS - 2)
    table = jnp.where(col_ok[None, None], jnp.take(rpb, jnp.asarray(dc), axis=2), NEG_BIAS)
    masked = jnp.full((HEADS, GRID_W, GRID_W), NEG_BIAS, F32)
    n_blocks = GRID_ROWS // A_QROWS
    variants = []
    for i in (0, 1, n_blocks - 1):
        ws = A_KCHUNK * min(max(i - 1, 0), n_blocks - A_NKC)
        cols = []
        for a in range(A_QROWS):
            qr = A_QROWS * i + a
            rs = min(max(qr - WIN_ROWS // 2, 0), GRID_ROWS - WIN_ROWS)
            blocks = []
            for b in range(A_KROWS):
                kr = ws + b
                blocks.append(table[:, kr - qr + WIN_ROWS - 1] if rs <= kr < rs + WIN_ROWS else masked)
            cols.append(jnp.concatenate(blocks, axis=1))
        variants.append(jnp.concatenate(cols, axis=2))
    return jnp.stack(variants)


def _gqa_kernel(q_ref, k_ref, vt_ref, gate_ref, o_ref, vt_ext_ref, st_ref, m_ref):
    blocks, qgroup, _ = q_ref.shape
    units = blocks * (qgroup // B_TQ)

    @pl.when(pl.program_id(2) == 0)
    def _():
        ones = jnp.ones((ONES_ROWS, SEQ), BF16)
        for hh in range(2):
            vt_ext_ref[hh] = jnp.concatenate([vt_ref[0, hh * HEAD_DIM:(hh + 1) * HEAD_DIM, :], ones], axis=0)

    masks = _head_masks()

    def rows(n):
        start = (n // blocks) * B_TQ
        return pl.ds(start if isinstance(n, int) else pl.multiple_of(start, B_TQ), B_TQ)

    def scores(n, slot):
        q2 = q_ref[n % blocks, rows(n), :]
        qm = [q2 * masks[hh] for hh in range(2)]
        return _score_steps(st_ref, m_ref, slot, B_TKC, SEQ // B_TKC,
                            lambda hh, keys: _dot_nt(k_ref[0, keys, :], qm[hh]))

    def finish(n, slot):
        def emit(ot):
            g = gate_ref[n % blocks, rows(n), :].astype(F32)
            o_ref[n % blocks, rows(n), :] = (ot.T * _silu(g)).astype(BF16)

        return _softmax_pv_steps(st_ref, m_ref, slot, B_TKC, SEQ // B_TKC,
                                 lambda hh, c: vt_ext_ref[hh, :, c * B_TKC:(c + 1) * B_TKC], emit)

    _pipeline(units, st_ref.shape[0], B_AHEAD, scores, finish)


def _gqa(q, k, vt, gate):
    bsz = k.shape[0]
    blocks_per_pair = HEAD_PAIRS // (KV_HEADS_B // 2)
    groups = SEQ // B_QGROUP
    q_spec = pl.BlockSpec((blocks_per_pair, B_QGROUP, LANES), lambda b, j, g: (j, b * groups + g, 0))
    return pl.pallas_call(
        _gqa_kernel,
        grid=(bsz, KV_HEADS_B // 2, groups),
        in_specs=[q_spec,
                  pl.BlockSpec((1, SEQ, LANES), lambda b, j, g: (b, 0, j)),
                  pl.BlockSpec((1, LANES, SEQ), lambda b, j, g: (b, j, 0)),
                  q_spec],
        out_specs=q_spec,
        out_shape=jax.ShapeDtypeStruct(q.shape, BF16),
        scratch_shapes=[pltpu.VMEM((2, HEAD_DIM + ONES_ROWS, SEQ), BF16),
                        pltpu.VMEM((B_SLOTS, 2, SEQ, B_TQ), F32),
                        pltpu.VMEM((B_SLOTS, 2, 1, B_TQ), F32)],
        compiler_params=pltpu.CompilerParams(
            dimension_semantics=("parallel", "parallel", "arbitrary"),
            vmem_limit_bytes=VMEM_LIMIT),
        name="gqa",
    )(q, k, vt, gate)


_B_HEAD_ORDER = [8 * (t // 4) + t % 4 + 4 * half for t in range(HEAD_PAIRS) for half in range(2)]


def _rope_tables():
    t = np.arange(SEQ)
    pos = np.stack([t // GRID_W, t % GRID_W], axis=1).astype(np.float32)
    lane = np.arange(LANES) % HEAD_DIM
    section = lane // ROPE_SECTION
    n_freq = ROPE_SECTION // 2
    inv = jnp.power(ROPE_THETA, -jnp.arange(n_freq, dtype=F32) * 2.0 / ROPE_SECTION)
    ang = jnp.asarray(pos)[:, section] * inv[lane % n_freq][None, :]
    sign = np.where(lane % ROPE_SECTION < n_freq, -1.0, 1.0).astype(np.float32)
    return jnp.cos(ang), jnp.sin(ang) * sign[None, :]


def kernel(x, p, norm_g, a_w_in, a_rpb, a_w_out, b_w_in, b_q_norm, b_k_norm, b_w_out, ple_norm_g, ple_w_gate,
           ple_w_proj, final_norm_g):
    bsz, seq, d = x.shape
    assert (seq, d) == (SEQ, D_MODEL)
    depth = p.shape[0]
    m = bsz * seq
    scale = math.log2(math.e) / math.sqrt(HEAD_DIM)
    xf = x.reshape(m, d)
    p_all = p.reshape(depth, m, PLE_DIM)
    cos, sin = _rope_tables()
    seg = np.kron(np.eye(4), np.ones((HEAD_DIM, HEAD_DIM)))
    seg = jnp.asarray(np.concatenate([seg, seg], axis=0), BF16)
    col_order = np.concatenate([np.arange(HEAD_DIM) + HEAD_DIM * h for h in _B_HEAD_ORDER])
    row2 = lambda v: v.reshape(1, -1)

    for i in range(depth):
        j = i // NUM_MIXERS
        if i % NUM_MIXERS == 0:
            w = a_w_in[j]
            wvt = w[:, 2 * D_MODEL:3 * D_MODEL].T.astype(BF16)
            w = jnp.concatenate([w[:, :D_MODEL] * scale, w[:, D_MODEL:2 * D_MODEL], w[:, 3 * D_MODEL:]],
                                axis=1).astype(BF16)
            q, k, vt, gate = _inproj_a(xf, row2(norm_g[i]), w, wvt)
            to3 = lambda t: t.reshape(bsz, seq, D_MODEL)
            a = _natten(to3(q), to3(k), vt, to3(gate), _natten_bias(a_rpb[j] * math.log2(math.e))).reshape(m, D_MODEL)
            wo = a_w_out[j].astype(BF16)
        else:
            w = b_w_in[j]
            qk_w = D_MODEL + KV_W_B
            wvt = w[:, qk_w:qk_w + KV_W_B].T.astype(BF16)
            w = jnp.concatenate([w[:, :D_MODEL][:, col_order], w[:, D_MODEL:qk_w],
                                 w[:, qk_w + KV_W_B:][:, col_order]], axis=1).astype(BF16)
            head_gain = jnp.concatenate([jnp.tile(b_q_norm[j] * scale, HEADS), jnp.tile(b_k_norm[j], KV_HEADS_B)])
            q, k, vt, gate = _inproj_b(xf, row2(norm_g[i]), w, wvt, row2(head_gain), cos, sin, seg)
            a = _gqa(q, k.reshape(bsz, seq, KV_W_B), vt, gate)
            wo = b_w_out[j][col_order, :].astype(BF16)
        xf = _outproj(a, xf, p_all, i, wo, row2(ple_norm_g[i]), ple_w_gate[i].astype(BF16),
                      ple_w_proj[i].astype(BF16), row2(final_norm_g), final_norm=(i == depth - 1))
    return xf.reshape(bsz, seq, d)
```

```python
import functools
import math

import jax
import jax.numpy as jnp
import numpy as np
from jax import lax
from jax.experimental import pallas as pl
from jax.experimental.pallas import tpu as pltpu

D_MODEL = 1024
SEQ = 4096
GRID_W = 64
GRID_ROWS = SEQ // GRID_W
HEAD_DIM = 64
HEADS = D_MODEL // HEAD_DIM
KV_HEADS_B = 4
KV_W_B = KV_HEADS_B * HEAD_DIM
WIN_ROWS = 8
WIN_COLS = 16
ROPE_THETA = 10000.0
ROPE_SECTION = HEAD_DIM // 2
PLE_DIM = 256
NORM_EPS = 1e-6
NUM_MIXERS = 2

LANES = 128
HEAD_PAIRS = HEADS // 2
NEG_BIAS = -1e30
ONES_ROWS = 16

ROW_TILE = 512
A_QROWS = 4
A_KROWS = 12
A_KCHUNK = 4
A_TQ = A_QROWS * GRID_W
A_TK = A_KROWS * GRID_W
A_TKC = A_KCHUNK * GRID_W
A_NKC = A_KROWS // A_KCHUNK
A_BATCH = 2
A_SLOTS, A_AHEAD = 4, 2
B_SLOTS, B_AHEAD = 2, 1
B_UNROLL = 4
SCORE_BOUND = 50.0
B_TQ = 256
B_QGROUP = 2048
B_TKC = 512
VMEM_LIMIT = 56 * 1024 * 1024

BF16 = jnp.bfloat16
F32 = jnp.float32


def _rms(x, g):
    ms = jnp.mean(x * x, axis=-1, keepdims=True)
    return x * lax.rsqrt(ms + NORM_EPS) * g


def _dot(a, b):
    return jnp.dot(a, b, preferred_element_type=F32)


def _dot_nt(a, b):
    return lax.dot_general(a, b, (((1,), (1,)), ((), ())), preferred_element_type=F32)


def _head_masks():
    lo = (lax.broadcasted_iota(jnp.int32, (1, LANES), 1) < HEAD_DIM).astype(BF16)
    return lo, 1 - lo


def _silu(g):
    return g * jax.nn.sigmoid(g)


def _score_steps(st_ref, m_ref, slot, key_chunk, n_chunks, chunk_scores):
    maxes = ([], [])

    def chunk(c):
        keys = pl.ds(c * key_chunk, key_chunk)
        for hh in range(2):
            st = chunk_scores(hh, keys)
            st_ref[slot, hh, keys, :] = st
            maxes[hh].append(jnp.max(st, axis=0, keepdims=True))

    def done():
        for hh in range(2):
            m_ref[slot, hh] = functools.reduce(jnp.maximum, maxes[hh])

    return [functools.partial(chunk, c) for c in range(n_chunks)] + [done]


def _softmax_pv_steps(st_ref, m_ref, slot, key_chunk, n_chunks, vt_chunk, emit):
    accs = [None, None]

    def chunk(c):
        keys = pl.ds(c * key_chunk, key_chunk)
        for hh in range(2):
            pt = jnp.exp2(st_ref[slot, hh, keys, :] - m_ref[slot, hh]).astype(BF16)
            pv = _dot(vt_chunk(hh, c), pt)
            accs[hh] = pv if accs[hh] is None else accs[hh] + pv

    def done():
        emit(jnp.concatenate([a[:HEAD_DIM] / a[HEAD_DIM:HEAD_DIM + 1] for a in accs], axis=0))

    return [functools.partial(chunk, c) for c in range(n_chunks)] + [done]


def _pipeline(units, slots, ahead, scores, finish):
    assert units % slots == 0 and 0 < ahead < slots

    def run(*step_lists):
        for steps in zip(*step_lists):
            for step in reversed(steps):
                step()

    def trip(base, last):
        for k in range(slots):
            if not last or k + ahead < slots:
                run(scores(base + k + ahead, (k + ahead) % slots), finish(base + k, k))
            else:
                run(finish(base + k, k))

    for n in range(ahead):
        run(scores(n, n))

    def body(t, carry):
        trip(t * slots, False)
        return carry

    lax.fori_loop(0, units // slots - 1, body, 0)
    trip(units - slots, True)


def _inproj_a_kernel(x_ref, g_ref, w_ref, wvt_ref, q_ref, k_ref, vt_ref, gate_ref):
    h = _rms(x_ref[...], g_ref[...]).astype(BF16)
    for c, o_ref in enumerate((q_ref, k_ref, gate_ref)):
        o_ref[...] = _dot(h, w_ref[:, c * D_MODEL:(c + 1) * D_MODEL]).astype(BF16)
    vt = _dot_nt(wvt_ref[...], h).astype(BF16)
    for c in range(vt_ref.shape[1]):
        vt_ref[0, c] = vt[:, c * A_TKC:(c + 1) * A_TKC]


def _inproj_a(x, g, w, wvt):
    m = x.shape[0]
    out = jax.ShapeDtypeStruct((m, D_MODEL), BF16)
    tiles = SEQ // ROW_TILE
    fixed = lambda i: (0, 0)
    row_spec = pl.BlockSpec((ROW_TILE, D_MODEL), lambda i: (i, 0))
    return pl.pallas_call(
        _inproj_a_kernel,
        grid=(m // ROW_TILE,),
        in_specs=[row_spec,
                  pl.BlockSpec((1, D_MODEL), fixed),
                  pl.BlockSpec((D_MODEL, 3 * D_MODEL), fixed),
                  pl.BlockSpec((D_MODEL, D_MODEL), fixed)],
        out_specs=[row_spec, row_spec,
                   pl.BlockSpec((1, ROW_TILE // A_TKC, D_MODEL, A_TKC), lambda i: (i // tiles, i % tiles, 0, 0)),
                   row_spec],
        out_shape=[out, out, jax.ShapeDtypeStruct((m // SEQ, SEQ // A_TKC, D_MODEL, A_TKC), BF16), out],
        compiler_params=pltpu.CompilerParams(dimension_semantics=("parallel",), vmem_limit_bytes=VMEM_LIMIT),
        name="inproj_a",
    )(x, g, w, wvt)


def _inproj_b_kernel(x_ref, g_ref, w_ref, wvt_ref, hg_ref, cos_ref, sin_ref, seg_ref, q_ref, k_ref, vt_ref,
                     gate_ref):
    qk_w = D_MODEL + KV_W_B
    h = _rms(x_ref[...], g_ref[...]).astype(BF16)
    seg = seg_ref[...]
    cos = cos_ref[...]
    sin = sin_ref[...]
    first_half = (lax.broadcasted_iota(jnp.int32, (1, LANES), 1) % ROPE_SECTION) < (ROPE_SECTION // 2)
    yqk = _dot(h, w_ref[:, :qk_w])
    wide = 2 * LANES
    for pb in range(qk_w // wide):
        y = yqk[:, pb * wide:(pb + 1) * wide]
        ss = y * y
        hi = ss.astype(BF16)
        lo = (ss - hi.astype(F32)).astype(BF16)
        ms = _dot(jnp.concatenate([hi, lo], axis=1), seg) * (1.0 / HEAD_DIM)
        yn2 = y * lax.rsqrt(ms + NORM_EPS) * hg_ref[:, pb * wide:(pb + 1) * wide]
        for half in range(2):
            c = 2 * pb + half
            yn = yn2[:, half * LANES:(half + 1) * LANES]
            partner = jnp.where(first_half, pltpu.roll(yn, LANES - ROPE_SECTION // 2, 1),
                                pltpu.roll(yn, ROPE_SECTION // 2, 1))
            r = (yn * cos + partner * sin).astype(BF16)
            if c < D_MODEL // LANES:
                q_ref[c] = r
            else:
                k_ref[:, (c - D_MODEL // LANES) * LANES:(c - D_MODEL // LANES + 1) * LANES] = r
    vt_ref[0] = _dot_nt(wvt_ref[...], h).astype(BF16)
    gate = _dot(h, w_ref[:, qk_w:])
    for c in range(D_MODEL // LANES):
        gate_ref[c] = gate[:, c * LANES:(c + 1) * LANES].astype(BF16)


def _inproj_b(x, g, w, wvt, head_gain, cos, sin, seg):
    m = x.shape[0]
    n_w = 2 * D_MODEL + KV_W_B
    tiles = SEQ // ROW_TILE
    row = lambda i: (i, 0)
    fixed = lambda i: (0, 0)
    pos = lambda i: (i % tiles, 0)
    wide = jax.ShapeDtypeStruct((D_MODEL // LANES, m, LANES), BF16)
    wide_spec = pl.BlockSpec((D_MODEL // LANES, ROW_TILE, LANES), lambda i: (0, i, 0))
    return pl.pallas_call(
        _inproj_b_kernel,
        grid=(m // ROW_TILE,),
        in_specs=[pl.BlockSpec((ROW_TILE, D_MODEL), row),
                  pl.BlockSpec((1, D_MODEL), fixed),
                  pl.BlockSpec((D_MODEL, n_w), fixed),
                  pl.BlockSpec((KV_W_B, D_MODEL), fixed),
                  pl.BlockSpec((1, D_MODEL + KV_W_B), fixed),
                  pl.BlockSpec((ROW_TILE, LANES), pos),
                  pl.BlockSpec((ROW_TILE, LANES), pos),
                  pl.BlockSpec((4 * LANES, 2 * LANES), fixed)],
        out_specs=[wide_spec,
                   pl.BlockSpec((ROW_TILE, KV_W_B), row),
                   pl.BlockSpec((1, KV_W_B, ROW_TILE), lambda i: (i // tiles, 0, i % tiles)),
                   wide_spec],
        out_shape=[wide, jax.ShapeDtypeStruct((m, KV_W_B), BF16),
                   jax.ShapeDtypeStruct((m // SEQ, KV_W_B, SEQ), BF16), wide],
        compiler_params=pltpu.CompilerParams(dimension_semantics=("parallel",), vmem_limit_bytes=VMEM_LIMIT),
        name="inproj_b",
    )(x, g, w, wvt, head_gain, cos, sin, seg)


def _outproj_kernel(a_ref, x_ref, p_ref, wo_ref, pg_ref, wg_ref, wp_ref, fg_ref, o_ref, *, final_norm):
    if len(a_ref.shape) == 3:
        a = jnp.concatenate([a_ref[c] for c in range(a_ref.shape[0])], axis=1)
    else:
        a = a_ref[...]
    x1 = x_ref[...] + _dot(a, wo_ref[...])
    hn = _rms(x1, pg_ref[...]).astype(BF16)
    gate = jax.nn.sigmoid(_dot(hn, wg_ref[...]))
    x2 = x1 + gate * _dot(p_ref[...].astype(BF16), wp_ref[...])
    o_ref[...] = _rms(x2, fg_ref[...]) if final_norm else x2


def _outproj(a, x, p_all, layer, wo, pg, wg, wp, fg, final_norm):
    m = x.shape[0]
    row = lambda i: (i, 0)
    fixed = lambda i: (0, 0)
    return pl.pallas_call(
        functools.partial(_outproj_kernel, final_norm=final_norm),
        grid=(m // ROW_TILE,),
        in_specs=[pl.BlockSpec((ROW_TILE, D_MODEL), row) if a.ndim == 2 else
                  pl.BlockSpec((a.shape[0], ROW_TILE, LANES), lambda i: (0, i, 0)),
                  pl.BlockSpec((ROW_TILE, D_MODEL), row),
                  pl.BlockSpec((None, ROW_TILE, PLE_DIM), lambda i: (layer, i, 0)),
                  pl.BlockSpec((D_MODEL, D_MODEL), fixed),
                  pl.BlockSpec((1, D_MODEL), fixed),
                  pl.BlockSpec((D_MODEL, D_MODEL), fixed),
                  pl.BlockSpec((PLE_DIM, D_MODEL), fixed),
                  pl.BlockSpec((1, D_MODEL), fixed)],
        out_specs=pl.BlockSpec((ROW_TILE, D_MODEL), row),
        out_shape=jax.ShapeDtypeStruct((m, D_MODEL), F32),
        compiler_params=pltpu.CompilerParams(dimension_semantics=("parallel",), vmem_limit_bytes=VMEM_LIMIT),
        name="outproj_final" if final_norm else "outproj",
    )(a, x, p_all, wo, pg, wg, wp, fg)


def _natten_kernel(q_ref, k_ref, vt_ref, gate_ref, bias_ref, o_ref, st_ref, m_ref):
    n_blocks = GRID_ROWS // A_QROWS
    masks = _head_masks()
    ones = jnp.ones((ONES_ROWS, A_TKC), BF16)

    def where(n):
        n = jnp.asarray(n, jnp.int32)
        bb, i = n // n_blocks, n % n_blocks
        first_chunk = jnp.clip(i - 1, 0, n_blocks - A_NKC)
        variant = jnp.where(i == 0, 0, jnp.where(i == n_blocks - 1, 2, 1))
        return bb, pl.ds(pl.multiple_of(i * A_TQ, A_TQ), A_TQ), first_chunk, variant

    def scores(n, slot):
        bb, q_rows, first_chunk, variant = where(n)
        q2 = q_ref[bb, q_rows, :]
        qm = [q2 * masks[hh] for hh in range(2)]

        def chunk_scores(hh, keys):
            k_rows = pl.ds(pl.multiple_of(first_chunk * A_TKC + keys.start, A_TKC), A_TKC)
            return _dot_nt(k_ref[bb, k_rows, :], qm[hh]) + bias_ref[variant, hh, keys, :]

        return _score_steps(st_ref, m_ref, slot, A_TKC, A_NKC, chunk_scores)

    def finish(n, slot):
        bb, q_rows, first_chunk, _ = where(n)

        def vt_chunk(hh, c):
            return jnp.concatenate([vt_ref[bb, first_chunk + c, hh * HEAD_DIM:(hh + 1) * HEAD_DIM, :], ones], axis=0)

        def emit(ot):
            o_ref[bb, q_rows, :] = (ot.T * _silu(gate_ref[bb, q_rows, :].astype(F32))).astype(BF16)

        return _softmax_pv_steps(st_ref, m_ref, slot, A_TKC, A_NKC, vt_chunk, emit)

    _pipeline(q_ref.shape[0] * n_blocks, st_ref.shape[0], A_AHEAD, scores, finish)


def _natten(q, k, vt, gate, bias):
    bsz = q.shape[0]
    tok_spec = pl.BlockSpec((A_BATCH, SEQ, LANES), lambda p, g: (g, 0, p))
    return pl.pallas_call(
        _natten_kernel,
        grid=(HEAD_PAIRS, bsz // A_BATCH),
        in_specs=[tok_spec, tok_spec,
                  pl.BlockSpec((A_BATCH, SEQ // A_TKC, LANES, A_TKC), lambda p, g: (g, 0, p, 0)),
                  tok_spec,
                  pl.BlockSpec((3, 2, A_TK, A_TQ), lambda p, g: (0, p, 0, 0))],
        out_specs=tok_spec,
        out_shape=jax.ShapeDtypeStruct(q.shape, BF16),
        scratch_shapes=[pltpu.VMEM((A_SLOTS, 2, A_TK, A_TQ), F32), pltpu.VMEM((A_SLOTS, 2, 1, A_TQ), F32)],
        compiler_params=pltpu.CompilerParams(dimension_semantics=("parallel", "parallel"),
                                             vmem_limit_bytes=VMEM_LIMIT),
        name="natten",
    )(q, k, vt, gate, bias)


def _natten_bias(rpb):
    qc = np.arange(GRID_W)
    kc = np.arange(GRID_W)
    cs = np.clip(qc - WIN_COLS // 2, 0, GRID_W - WIN_COLS)
    col_ok = (kc[:, None] >= cs[None, :]) & (kc[:, None] < cs[None, :] + WIN_COLS)
    dc = np.clip(kc[:, None] - qc[None, :] + WIN_COLS - 1, 0, 2 * WIN_COLS - 2)
    table = jnp.where(col_ok[None, None], jnp.take(rpb, jnp.asarray(dc), axis=2), NEG_BIAS)
    n_blocks = GRID_ROWS // A_QROWS
    dr = np.zeros((3, A_KROWS, A_QROWS), np.int32)
    row_ok = np.zeros((3, A_KROWS, A_QROWS), bool)
    for v, i in enumerate((0, 1, n_blocks - 1)):
        ws = A_KCHUNK * min(max(i - 1, 0), n_blocks - A_NKC)
        for a in range(A_QROWS):
            qr = A_QROWS * i + a
            rs = min(max(qr - WIN_ROWS // 2, 0), GRID_ROWS - WIN_ROWS)
            for b in range(A_KROWS):
                row_ok[v, b, a] = rs <= ws + b < rs + WIN_ROWS
                dr[v, b, a] = min(max(ws + b - qr + WIN_ROWS - 1, 0), 2 * WIN_ROWS - 2)
    slabs = jnp.take(table, jnp.asarray(dr), axis=1)
    slabs = jnp.where(row_ok[None, :, :, :, None, None], slabs, NEG_BIAS)
    return slabs.transpose(1, 0, 2, 4, 3, 5).reshape(3, HEADS, A_TK, A_TQ)


def _gqa_kernel(q_ref, k_ref, vt_ref, gate_ref, o_ref, vt_ext_ref, *scratch, bounded):
    blocks, qgroup, _ = q_ref.shape
    units = blocks * (qgroup // B_TQ)
    n_chunks = SEQ // B_TKC

    @pl.when(pl.program_id(2) == 0)
    def _():
        ones = jnp.ones((ONES_ROWS, SEQ), BF16)
        for hh in range(2):
            vt_ext_ref[hh] = jnp.concatenate([vt_ref[0, hh * HEAD_DIM:(hh + 1) * HEAD_DIM, :], ones], axis=0)

    masks = _head_masks()

    def rows(n):
        start = (n // blocks) * B_TQ
        return pl.ds(start if isinstance(n, int) else pl.multiple_of(start, B_TQ), B_TQ)

    def emit(n, ot):
        g = gate_ref[n % blocks, rows(n), :].astype(F32)
        o_ref[n % blocks, rows(n), :] = (ot.T * _silu(g)).astype(BF16)

    def masked_q(n):
        q2 = q_ref[n % blocks, rows(n), :]
        return [q2 * masks[hh] for hh in range(2)]

    def vt_chunk(hh, c):
        return vt_ext_ref[hh, :, c * B_TKC:(c + 1) * B_TKC]

    if bounded:
        def unit(n):
            qm = masked_q(n)
            accs = [None, None]
            st = {}

            def qk(c):
                for hh in range(2):
                    st[c, hh] = _dot_nt(k_ref[0, c * B_TKC:(c + 1) * B_TKC, :], qm[hh])

            qk(0)
            for c in range(n_chunks):
                if c + 1 < n_chunks:
                    qk(c + 1)
                for hh in range(2):
                    pv = _dot(vt_chunk(hh, c), jnp.exp2(st.pop((c, hh))).astype(BF16))
                    accs[hh] = pv if accs[hh] is None else accs[hh] + pv
            emit(n, jnp.concatenate([a[:HEAD_DIM] / a[HEAD_DIM:HEAD_DIM + 1] for a in accs], axis=0))

        def body(t, carry):
            for k in range(B_UNROLL):
                unit(t * B_UNROLL + k)
            return carry

        lax.fori_loop(0, units // B_UNROLL, body, 0)
        return

    st_ref, m_ref = scratch

    def scores(n, slot):
        qm = masked_q(n)
        return _score_steps(st_ref, m_ref, slot, B_TKC, n_chunks,
                            lambda hh, keys: _dot_nt(k_ref[0, keys, :], qm[hh]))

    def finish(n, slot):
        return _softmax_pv_steps(st_ref, m_ref, slot, B_TKC, n_chunks, vt_chunk, functools.partial(emit, n))

    _pipeline(units, st_ref.shape[0], B_AHEAD, scores, finish)


def _gqa(q, k, vt, gate, bounded):
    bsz = k.shape[0]
    blocks_per_pair = HEAD_PAIRS // (KV_HEADS_B // 2)
    groups = SEQ // B_QGROUP
    q_spec = pl.BlockSpec((blocks_per_pair, B_QGROUP, LANES), lambda b, j, g: (j, b * groups + g, 0))
    scratch = [pltpu.VMEM((2, HEAD_DIM + ONES_ROWS, SEQ), BF16)]
    if not bounded:
        scratch += [pltpu.VMEM((B_SLOTS, 2, SEQ, B_TQ), F32), pltpu.VMEM((B_SLOTS, 2, 1, B_TQ), F32)]
    return pl.pallas_call(
        functools.partial(_gqa_kernel, bounded=bounded),
        grid=(bsz, KV_HEADS_B // 2, groups),
        in_specs=[q_spec,
                  pl.BlockSpec((1, SEQ, LANES), lambda b, j, g: (b, 0, j)),
                  pl.BlockSpec((1, LANES, SEQ), lambda b, j, g: (b, j, 0)),
                  q_spec],
        out_specs=q_spec,
        out_shape=jax.ShapeDtypeStruct(q.shape, BF16),
        scratch_shapes=scratch,
        compiler_params=pltpu.CompilerParams(
            dimension_semantics=("parallel", "parallel", "arbitrary"),
            vmem_limit_bytes=VMEM_LIMIT),
        name="gqa_bounded" if bounded else "gqa",
    )(q, k, vt, gate)


def _pair_heads(w, axis):
    lead, trail = w.shape[:axis], w.shape[axis + 1:]
    group = HEADS // KV_HEADS_B
    w = w.reshape(*lead, KV_HEADS_B // 2, 2, group, HEAD_DIM, *trail)
    w = jnp.swapaxes(w, len(lead) + 1, len(lead) + 2)
    return w.reshape(*lead, D_MODEL, *trail)


def _rope_tables():
    t = np.arange(SEQ)
    pos = np.stack([t // GRID_W, t % GRID_W], axis=1).astype(np.float32)
    lane = np.arange(LANES) % HEAD_DIM
    section = lane // ROPE_SECTION
    n_freq = ROPE_SECTION // 2
    inv = jnp.power(ROPE_THETA, -jnp.arange(n_freq, dtype=F32) * 2.0 / ROPE_SECTION)
    ang = jnp.asarray(pos)[:, section] * inv[lane % n_freq][None, :]
    sign = np.where(lane % ROPE_SECTION < n_freq, -1.0, 1.0).astype(np.float32)
    return jnp.cos(ang), jnp.sin(ang) * sign[None, :]


def kernel(x, p, norm_g, a_w_in, a_rpb, a_w_out, b_w_in, b_q_norm, b_k_norm, b_w_out, ple_norm_g, ple_w_gate,
           ple_w_proj, final_norm_g):
    bsz, seq, d = x.shape
    assert (seq, d) == (SEQ, D_MODEL)
    depth = p.shape[0]
    m = bsz * seq
    scale = math.log2(math.e) / math.sqrt(HEAD_DIM)
    xf = x.reshape(m, d)
    p_all = p.reshape(depth, m, PLE_DIM)
    cos, sin = _rope_tables()
    seg = np.kron(np.eye(4), np.ones((HEAD_DIM, HEAD_DIM)))
    seg = jnp.asarray(np.concatenate([seg, seg], axis=0), BF16)
    row2 = lambda v: v.reshape(1, -1)

    for i in range(depth):
        j = i // NUM_MIXERS
        if i % NUM_MIXERS == 0:
            w = a_w_in[j]
            wvt = w[:, 2 * D_MODEL:3 * D_MODEL].T.astype(BF16)
            w = jnp.concatenate([w[:, :D_MODEL] * scale, w[:, D_MODEL:2 * D_MODEL], w[:, 3 * D_MODEL:]],
                                axis=1).astype(BF16)
            q, k, vt, gate = _inproj_a(xf, row2(norm_g[i]), w, wvt)
            to3 = lambda t: t.reshape(bsz, seq, D_MODEL)
            a = _natten(to3(q), to3(k), vt, to3(gate), _natten_bias(a_rpb[j] * math.log2(math.e))).reshape(m, D_MODEL)
            wo = a_w_out[j].astype(BF16)
        else:
            w = b_w_in[j]
            qk_w = D_MODEL + KV_W_B
            wvt = w[:, qk_w:qk_w + KV_W_B].T.astype(BF16)
            w = jnp.concatenate([_pair_heads(w[:, :D_MODEL], 1), w[:, D_MODEL:qk_w],
                                 _pair_heads(w[:, qk_w + KV_W_B:], 1)], axis=1).astype(BF16)
            head_gain = jnp.concatenate([jnp.tile(b_q_norm[j] * scale, HEADS), jnp.tile(b_k_norm[j], KV_HEADS_B)])
            q, k, vt, gate = _inproj_b(xf, row2(norm_g[i]), w, wvt, row2(head_gain), cos, sin, seg)
            score_bound = HEAD_DIM * jnp.max(jnp.abs(b_q_norm[j] * scale)) * jnp.max(jnp.abs(b_k_norm[j]))
            k3 = k.reshape(bsz, seq, KV_W_B)
            a = lax.cond(score_bound <= SCORE_BOUND,
                         lambda: _gqa(q, k3, vt, gate, bounded=True),
                         lambda: _gqa(q, k3, vt, gate, bounded=False))
            wo = _pair_heads(b_w_out[j], 0).astype(BF16)
        xf = _outproj(a, xf, p_all, i, wo, row2(ple_norm_g[i]), ple_w_gate[i].astype(BF16),
                      ple_w_proj[i].astype(BF16), row2(final_norm_g), final_norm=(i == depth - 1))
    return xf.reshape(bsz, seq, d)
```

```python
import functools
import math

import jax
import jax.numpy as jnp
import numpy as np
from jax import lax
from jax.experimental import pallas as pl
from jax.experimental.pallas import tpu as pltpu

D_MODEL = 1024
SEQ = 4096
GRID_W = 64
GRID_ROWS = SEQ // GRID_W
HEAD_DIM = 64
HEADS = D_MODEL // HEAD_DIM
KV_HEADS_B = 4
KV_W_B = KV_HEADS_B * HEAD_DIM
WIN_ROWS = 8
WIN_COLS = 16
ROPE_THETA = 10000.0
ROPE_SECTION = HEAD_DIM // 2
PLE_DIM = 256
NORM_EPS = 1e-6
NUM_MIXERS = 2

LANES = 128
HEAD_PAIRS = HEADS // 2
NEG_BIAS = -1e30
ONES_ROWS = 16

ROW_TILE = 512
A_QROWS = 4
A_KROWS = 12
A_KCHUNK = 4
A_TQ = A_QROWS * GRID_W
A_TK = A_KROWS * GRID_W
A_TKC = A_KCHUNK * GRID_W
A_NKC = A_KROWS // A_KCHUNK
A_BATCH = 2
A_SLOTS, A_AHEAD = 4, 2
B_SLOTS, B_AHEAD = 2, 1
B_UNROLL = 4
SCORE_BOUND = 50.0
B_TQ = 256
B_QGROUP = 2048
B_TKC = 2048
VMEM_LIMIT = 56 * 1024 * 1024

BF16 = jnp.bfloat16
F32 = jnp.float32


def _rms(x, g):
    ms = jnp.mean(x * x, axis=-1, keepdims=True)
    return x * lax.rsqrt(ms + NORM_EPS) * g


def _dot(a, b):
    return jnp.dot(a, b, preferred_element_type=F32)


def _dot_nt(a, b):
    return lax.dot_general(a, b, (((1,), (1,)), ((), ())), preferred_element_type=F32)


def _head_masks():
    lo = (lax.broadcasted_iota(jnp.int32, (1, LANES), 1) < HEAD_DIM).astype(BF16)
    return lo, 1 - lo


def _silu(g):
    return g * jax.nn.sigmoid(g)


def _score_steps(st_ref, m_ref, slot, key_chunk, n_chunks, chunk_scores):
    maxes = ([], [])

    def chunk(c):
        keys = pl.ds(c * key_chunk, key_chunk)
        for hh in range(2):
            st = chunk_scores(hh, keys)
            st_ref[slot, hh, keys, :] = st
            maxes[hh].append(jnp.max(st, axis=0, keepdims=True))

    def done():
        for hh in range(2):
            m_ref[slot, hh] = functools.reduce(jnp.maximum, maxes[hh])

    return [functools.partial(chunk, c) for c in range(n_chunks)] + [done]


def _softmax_pv_steps(st_ref, m_ref, slot, key_chunk, n_chunks, vt_chunk, emit):
    accs = [None, None]

    def chunk(c):
        keys = pl.ds(c * key_chunk, key_chunk)
        for hh in range(2):
            pt = jnp.exp2(st_ref[slot, hh, keys, :] - m_ref[slot, hh]).astype(BF16)
            pv = _dot(vt_chunk(hh, c), pt)
            accs[hh] = pv if accs[hh] is None else accs[hh] + pv

    def done():
        emit(jnp.concatenate([a[:HEAD_DIM] / a[HEAD_DIM:HEAD_DIM + 1] for a in accs], axis=0))

    return [functools.partial(chunk, c) for c in range(n_chunks)] + [done]


def _pipeline(units, slots, ahead, scores, finish):
    assert units % slots == 0 and 0 < ahead < slots

    def run(*step_lists):
        for steps in zip(*step_lists):
            for step in reversed(steps):
                step()

    def trip(base, last):
        for k in range(slots):
            if not last or k + ahead < slots:
                run(scores(base + k + ahead, (k + ahead) % slots), finish(base + k, k))
            else:
                run(finish(base + k, k))

    for n in range(ahead):
        run(scores(n, n))

    def body(t, carry):
        trip(t * slots, False)
        return carry

    lax.fori_loop(0, units // slots - 1, body, 0)
    trip(units - slots, True)


def _inproj_a_kernel(x_ref, g_ref, w_ref, wvt_ref, q_ref, k_ref, vt_ref, gate_ref):
    h = _rms(x_ref[...], g_ref[...]).astype(BF16)
    for c, o_ref in enumerate((q_ref, k_ref, gate_ref)):
        o_ref[...] = _dot(h, w_ref[:, c * D_MODEL:(c + 1) * D_MODEL]).astype(BF16)
    vt = _dot_nt(wvt_ref[...], h).astype(BF16)
    for c in range(vt_ref.shape[1]):
        vt_ref[0, c] = vt[:, c * A_TKC:(c + 1) * A_TKC]


def _inproj_a(x, g, w, wvt):
    m = x.shape[0]
    out = jax.ShapeDtypeStruct((m, D_MODEL), BF16)
    tiles = SEQ // ROW_TILE
    fixed = lambda i: (0, 0)
    row_spec = pl.BlockSpec((ROW_TILE, D_MODEL), lambda i: (i, 0))
    return pl.pallas_call(
        _inproj_a_kernel,
        grid=(m // ROW_TILE,),
        in_specs=[row_spec,
                  pl.BlockSpec((1, D_MODEL), fixed),
                  pl.BlockSpec((D_MODEL, 3 * D_MODEL), fixed),
                  pl.BlockSpec((D_MODEL, D_MODEL), fixed)],
        out_specs=[row_spec, row_spec,
                   pl.BlockSpec((1, ROW_TILE // A_TKC, D_MODEL, A_TKC), lambda i: (i // tiles, i % tiles, 0, 0)),
                   row_spec],
        out_shape=[out, out, jax.ShapeDtypeStruct((m // SEQ, SEQ // A_TKC, D_MODEL, A_TKC), BF16), out],
        compiler_params=pltpu.CompilerParams(dimension_semantics=("parallel",), vmem_limit_bytes=VMEM_LIMIT),
        name="inproj_a",
    )(x, g, w, wvt)


def _inproj_b_kernel(x_ref, g_ref, w_ref, wvt_ref, hg_ref, cos_ref, sin_ref, seg_ref, q_ref, k_ref, vt_ref,
                     gate_ref):
    qk_w = D_MODEL + KV_W_B
    h = _rms(x_ref[...], g_ref[...]).astype(BF16)
    seg = seg_ref[...]
    cos = cos_ref[...]
    sin = sin_ref[...]
    first_half = (lax.broadcasted_iota(jnp.int32, (1, LANES), 1) % ROPE_SECTION) < (ROPE_SECTION // 2)
    yqk = _dot(h, w_ref[:, :qk_w])
    wide = 2 * LANES
    for pb in range(qk_w // wide):
        y = yqk[:, pb * wide:(pb + 1) * wide]
        ss = y * y
        hi = ss.astype(BF16)
        lo = (ss - hi.astype(F32)).astype(BF16)
        ms = _dot(jnp.concatenate([hi, lo], axis=1), seg) * (1.0 / HEAD_DIM)
        yn2 = y * lax.rsqrt(ms + NORM_EPS) * hg_ref[:, pb * wide:(pb + 1) * wide]
        for half in range(2):
            c = 2 * pb + half
            yn = yn2[:, half * LANES:(half + 1) * LANES]
            partner = jnp.where(first_half, pltpu.roll(yn, LANES - ROPE_SECTION // 2, 1),
                                pltpu.roll(yn, ROPE_SECTION // 2, 1))
            r = (yn * cos + partner * sin).astype(BF16)
            if c < D_MODEL // LANES:
                q_ref[c] = r
            else:
                k_ref[:, (c - D_MODEL // LANES) * LANES:(c - D_MODEL // LANES + 1) * LANES] = r
    vt_ref[0] = _dot_nt(wvt_ref[...], h).astype(BF16)
    gate = _dot(h, w_ref[:, qk_w:])
    for c in range(D_MODEL // LANES):
        gate_ref[c] = gate[:, c * LANES:(c + 1) * LANES].astype(BF16)


def _inproj_b(x, g, w, wvt, head_gain, cos, sin, seg):
    m = x.shape[0]
    n_w = 2 * D_MODEL + KV_W_B
    tiles = SEQ // ROW_TILE
    row = lambda i: (i, 0)
    fixed = lambda i: (0, 0)
    pos = lambda i: (i % tiles, 0)
    wide = jax.ShapeDtypeStruct((D_MODEL // LANES, m, LANES), BF16)
    wide_spec = pl.BlockSpec((D_MODEL // LANES, ROW_TILE, LANES), lambda i: (0, i, 0))
    return pl.pallas_call(
        _inproj_b_kernel,
        grid=(m // ROW_TILE,),
        in_specs=[pl.BlockSpec((ROW_TILE, D_MODEL), row),
                  pl.BlockSpec((1, D_MODEL), fixed),
                  pl.BlockSpec((D_MODEL, n_w), fixed),
                  pl.BlockSpec((KV_W_B, D_MODEL), fixed),
                  pl.BlockSpec((1, D_MODEL + KV_W_B), fixed),
                  pl.BlockSpec((ROW_TILE, LANES), pos),
                  pl.BlockSpec((ROW_TILE, LANES), pos),
                  pl.BlockSpec((4 * LANES, 2 * LANES), fixed)],
        out_specs=[wide_spec,
                   pl.BlockSpec((ROW_TILE, KV_W_B), row),
                   pl.BlockSpec((1, KV_W_B, ROW_TILE), lambda i: (i // tiles, 0, i % tiles)),
                   wide_spec],
        out_shape=[wide, jax.ShapeDtypeStruct((m, KV_W_B), BF16),
                   jax.ShapeDtypeStruct((m // SEQ, KV_W_B, SEQ), BF16), wide],
        compiler_params=pltpu.CompilerParams(dimension_semantics=("parallel",), vmem_limit_bytes=VMEM_LIMIT),
        name="inproj_b",
    )(x, g, w, wvt, head_gain, cos, sin, seg)


def _outproj_kernel(a_ref, x_ref, p_ref, wo_ref, pg_ref, wg_ref, wp_ref, fg_ref, o_ref, *, final_norm):
    if len(a_ref.shape) == 3:
        a = jnp.concatenate([a_ref[c] for c in range(a_ref.shape[0])], axis=1)
    else:
        a = a_ref[...]
    x1 = x_ref[...] + _dot(a, wo_ref[...])
    hn = _rms(x1, pg_ref[...]).astype(BF16)
    gate = jax.nn.sigmoid(_dot(hn, wg_ref[...]))
    x2 = x1 + gate * _dot(p_ref[...].astype(BF16), wp_ref[...])
    o_ref[...] = _rms(x2, fg_ref[...]) if final_norm else x2


def _outproj(a, x, p_all, layer, wo, pg, wg, wp, fg, final_norm):
    m = x.shape[0]
    row = lambda i: (i, 0)
    fixed = lambda i: (0, 0)
    return pl.pallas_call(
        functools.partial(_outproj_kernel, final_norm=final_norm),
        grid=(m // ROW_TILE,),
        in_specs=[pl.BlockSpec((ROW_TILE, D_MODEL), row) if a.ndim == 2 else
                  pl.BlockSpec((a.shape[0], ROW_TILE, LANES), lambda i: (0, i, 0)),
                  pl.BlockSpec((ROW_TILE, D_MODEL), row),
                  pl.BlockSpec((None, ROW_TILE, PLE_DIM), lambda i: (layer, i, 0)),
                  pl.BlockSpec((D_MODEL, D_MODEL), fixed),
                  pl.BlockSpec((1, D_MODEL), fixed),
                  pl.BlockSpec((D_MODEL, D_MODEL), fixed),
                  pl.BlockSpec((PLE_DIM, D_MODEL), fixed),
                  pl.BlockSpec((1, D_MODEL), fixed)],
        out_specs=pl.BlockSpec((ROW_TILE, D_MODEL), row),
        out_shape=jax.ShapeDtypeStruct((m, D_MODEL), F32),
        compiler_params=pltpu.CompilerParams(dimension_semantics=("parallel",), vmem_limit_bytes=VMEM_LIMIT),
        name="outproj_final" if final_norm else "outproj",
    )(a, x, p_all, wo, pg, wg, wp, fg)


def _natten_kernel(slab_ids_ref, q_ref, k_ref, vt_ref, gate_ref, bias_ref, o_ref, st_ref, m_ref):
    n_blocks = GRID_ROWS // A_QROWS
    masks = _head_masks()
    ones = jnp.ones((ONES_ROWS, A_TKC), BF16)

    def where(n):
        n = jnp.asarray(n, jnp.int32)
        bb, i = n // n_blocks, n % n_blocks
        first_chunk = jnp.clip(i - 1, 0, n_blocks - A_NKC)
        variant = jnp.where(i == 0, 0, jnp.where(i == n_blocks - 1, 2, 1))
        return bb, pl.ds(pl.multiple_of(i * A_TQ, A_TQ), A_TQ), first_chunk, variant

    def scores(n, slot):
        bb, q_rows, first_chunk, variant = where(n)
        q2 = q_ref[bb, q_rows, :]
        qm = [q2 * masks[hh] for hh in range(2)]

        def chunk_scores(hh, keys):
            k_rows = pl.ds(pl.multiple_of(first_chunk * A_TKC + keys.start, A_TKC), A_TKC)
            first_row = keys.start // GRID_W
            bias = jnp.concatenate(
                [jnp.concatenate([bias_ref[hh, slab_ids_ref[(variant * A_KROWS + first_row + r) * 2 + half]]
                                  for half in range(2)], axis=1) for r in range(A_KCHUNK)], axis=0)
            return _dot_nt(k_ref[bb, k_rows, :], qm[hh]) + bias

        return _score_steps(st_ref, m_ref, slot, A_TKC, A_NKC, chunk_scores)

    def finish(n, slot):
        bb, q_rows, first_chunk, _ = where(n)

        def vt_chunk(hh, c):
            return jnp.concatenate([vt_ref[bb, first_chunk + c, hh * HEAD_DIM:(hh + 1) * HEAD_DIM, :], ones], axis=0)

        def emit(ot):
            o_ref[bb, q_rows, :] = (ot.T * _silu(gate_ref[bb, q_rows, :].astype(F32))).astype(BF16)

        return _softmax_pv_steps(st_ref, m_ref, slot, A_TKC, A_NKC, vt_chunk, emit)

    _pipeline(q_ref.shape[0] * n_blocks, st_ref.shape[0], A_AHEAD, scores, finish)


def _natten(q, k, vt, gate, slabs, slab_ids):
    bsz = q.shape[0]
    tok_spec = pl.BlockSpec((A_BATCH, SEQ, LANES), lambda p, g: (g, 0, p))
    return pl.pallas_call(
        _natten_kernel,
        grid=(HEAD_PAIRS, bsz // A_BATCH),
        in_specs=[pl.BlockSpec(memory_space=pltpu.SMEM),
                  tok_spec, tok_spec,
                  pl.BlockSpec((A_BATCH, SEQ // A_TKC, LANES, A_TKC), lambda p, g: (g, 0, p, 0)),
                  tok_spec,
                  pl.BlockSpec((2,) + slabs.shape[1:], lambda p, g: (p, 0, 0, 0))],
        out_specs=tok_spec,
        out_shape=jax.ShapeDtypeStruct(q.shape, BF16),
        scratch_shapes=[pltpu.VMEM((A_SLOTS, 2, A_TK, A_TQ), F32), pltpu.VMEM((A_SLOTS, 2, 1, A_TQ), F32)],
        compiler_params=pltpu.CompilerParams(dimension_semantics=("parallel", "parallel"),
                                             vmem_limit_bytes=VMEM_LIMIT),
        name="natten",
    )(slab_ids, q, k, vt, gate, slabs)


def _natten_bias(rpb):
    qc = np.arange(GRID_W)
    kc = np.arange(GRID_W)
    cs = np.clip(qc - WIN_COLS // 2, 0, GRID_W - WIN_COLS)
    col_ok = (kc[:, None] >= cs[None, :]) & (kc[:, None] < cs[None, :] + WIN_COLS)
    dc = np.clip(kc[:, None] - qc[None, :] + WIN_COLS - 1, 0, 2 * WIN_COLS - 2)
    table = jnp.where(col_ok[None, None], jnp.take(rpb, jnp.asarray(dc), axis=2), NEG_BIAS)
    n_rel = 2 * WIN_ROWS - 1
    table = jnp.concatenate([table, jnp.full((HEADS, 1, GRID_W, GRID_W), NEG_BIAS, F32)], axis=1)
    n_blocks = GRID_ROWS // A_QROWS
    pairs, ids = [], []
    for i in (0, 1, n_blocks - 1):
        ws = A_KCHUNK * min(max(i - 1, 0), n_blocks - A_NKC)
        for b in range(A_KROWS):
            for half in range(A_QROWS // 2):
                rel = []
                for a in (2 * half, 2 * half + 1):
                    qr = A_QROWS * i + a
                    rs = min(max(qr - WIN_ROWS // 2, 0), GRID_ROWS - WIN_ROWS)
                    rel.append(ws + b - qr + WIN_ROWS - 1 if rs <= ws + b < rs + WIN_ROWS else n_rel)
                if tuple(rel) not in pairs:
                    pairs.append(tuple(rel))
                ids.append(pairs.index(tuple(rel)))
    left, right = (np.asarray([pr[side] for pr in pairs], np.int32) for side in range(2))
    slabs = jnp.concatenate([jnp.take(table, left, axis=1), jnp.take(table, right, axis=1)], axis=-1)
    return slabs, jnp.asarray(ids, jnp.int32)


def _gqa_kernel(q_ref, k_ref, vt_ref, gate_ref, o_ref, vt_ext_ref, *scratch, bounded):
    blocks, qgroup, _ = q_ref.shape
    units = blocks * (qgroup // B_TQ)
    n_chunks = SEQ // B_TKC

    @pl.when(pl.program_id(2) == 0)
    def _():
        ones = jnp.ones((ONES_ROWS, SEQ), BF16)
        for hh in range(2):
            vt_ext_ref[hh] = jnp.concatenate([vt_ref[0, hh * HEAD_DIM:(hh + 1) * HEAD_DIM, :], ones], axis=0)

    masks = _head_masks()

    def rows(n):
        start = (n // blocks) * B_TQ
        return pl.ds(start if isinstance(n, int) else pl.multiple_of(start, B_TQ), B_TQ)

    def emit(n, ot):
        g = gate_ref[n % blocks, rows(n), :].astype(F32)
        o_ref[n % blocks, rows(n), :] = (ot.T * _silu(g)).astype(BF16)

    def masked_q(n):
        q2 = q_ref[n % blocks, rows(n), :]
        return [q2 * masks[hh] for hh in range(2)]

    def vt_chunk(hh, c):
        return vt_ext_ref[hh, :, c * B_TKC:(c + 1) * B_TKC]

    if bounded:
        def unit(n):
            qm = masked_q(n)
            accs = [None, None]
            st = {}

            def qk(c):
                for hh in range(2):
                    st[c, hh] = _dot_nt(k_ref[0, c * B_TKC:(c + 1) * B_TKC, :], qm[hh])

            qk(0)
            for c in range(n_chunks):
                if c + 1 < n_chunks:
                    qk(c + 1)
                for hh in range(2):
                    pv = _dot(vt_chunk(hh, c), jnp.exp2(st.pop((c, hh))).astype(BF16))
                    accs[hh] = pv if accs[hh] is None else accs[hh] + pv
            emit(n, jnp.concatenate([a[:HEAD_DIM] / a[HEAD_DIM:HEAD_DIM + 1] for a in accs], axis=0))

        def body(t, carry):
            for k in range(B_UNROLL):
                unit(t * B_UNROLL + k)
            return carry

        lax.fori_loop(0, units // B_UNROLL, body, 0)
        return

    st_ref, m_ref = scratch

    def scores(n, slot):
        qm = masked_q(n)
        return _score_steps(st_ref, m_ref, slot, B_TKC, n_chunks,
                            lambda hh, keys: _dot_nt(k_ref[0, keys, :], qm[hh]))

    def finish(n, slot):
        return _softmax_pv_steps(st_ref, m_ref, slot, B_TKC, n_chunks, vt_chunk, functools.partial(emit, n))

    _pipeline(units, st_ref.shape[0], B_AHEAD, scores, finish)


def _gqa(q, k, vt, gate, bounded):
    bsz = k.shape[0]
    blocks_per_pair = HEAD_PAIRS // (KV_HEADS_B // 2)
    groups = SEQ // B_QGROUP
    q_spec = pl.BlockSpec((blocks_per_pair, B_QGROUP, LANES), lambda b, j, g: (j, b * groups + g, 0))
    scratch = [pltpu.VMEM((2, HEAD_DIM + ONES_ROWS, SEQ), BF16)]
    if not bounded:
        scratch += [pltpu.VMEM((B_SLOTS, 2, SEQ, B_TQ), F32), pltpu.VMEM((B_SLOTS, 2, 1, B_TQ), F32)]
    return pl.pallas_call(
        functools.partial(_gqa_kernel, bounded=bounded),
        grid=(bsz, KV_HEADS_B // 2, groups),
        in_specs=[q_spec,
                  pl.BlockSpec((1, SEQ, LANES), lambda b, j, g: (b, 0, j)),
                  pl.BlockSpec((1, LANES, SEQ), lambda b, j, g: (b, j, 0)),
                  q_spec],
        out_specs=q_spec,
        out_shape=jax.ShapeDtypeStruct(q.shape, BF16),
        scratch_shapes=scratch,
        compiler_params=pltpu.CompilerParams(
            dimension_semantics=("parallel", "parallel", "arbitrary"),
            vmem_limit_bytes=VMEM_LIMIT),
        name="gqa_bounded" if bounded else "gqa",
    )(q, k, vt, gate)


def _pair_heads(w, axis):
    lead, trail = w.shape[:axis], w.shape[axis + 1:]
    group = HEADS // KV_HEADS_B
    w = w.reshape(*lead, KV_HEADS_B // 2, 2, group, HEAD_DIM, *trail)
    w = jnp.swapaxes(w, len(lead) + 1, len(lead) + 2)
    return w.reshape(*lead, D_MODEL, *trail)


def _rope_tables():
    t = np.arange(SEQ)
    pos = np.stack([t // GRID_W, t % GRID_W], axis=1).astype(np.float32)
    lane = np.arange(LANES) % HEAD_DIM
    section = lane // ROPE_SECTION
    n_freq = ROPE_SECTION // 2
    inv = jnp.power(ROPE_THETA, -jnp.arange(n_freq, dtype=F32) * 2.0 / ROPE_SECTION)
    ang = jnp.asarray(pos)[:, section] * inv[lane % n_freq][None, :]
    sign = np.where(lane % ROPE_SECTION < n_freq, -1.0, 1.0).astype(np.float32)
    return jnp.cos(ang), jnp.sin(ang) * sign[None, :]


def kernel(x, p, norm_g, a_w_in, a_rpb, a_w_out, b_w_in, b_q_norm, b_k_norm, b_w_out, ple_norm_g, ple_w_gate,
           ple_w_proj, final_norm_g):
    bsz, seq, d = x.shape
    assert (seq, d) == (SEQ, D_MODEL)
    depth = p.shape[0]
    m = bsz * seq
    scale = math.log2(math.e) / math.sqrt(HEAD_DIM)
    xf = x.reshape(m, d)
    p_all = p.reshape(depth, m, PLE_DIM)
    cos, sin = _rope_tables()
    seg = np.kron(np.eye(4), np.ones((HEAD_DIM, HEAD_DIM)))
    seg = jnp.asarray(np.concatenate([seg, seg], axis=0), BF16)
    row2 = lambda v: v.reshape(1, -1)

    for i in range(depth):
        j = i // NUM_MIXERS
        if i % NUM_MIXERS == 0:
            w = a_w_in[j]
            wvt = w[:, 2 * D_MODEL:3 * D_MODEL].T.astype(BF16)
            w = jnp.concatenate([w[:, :D_MODEL] * scale, w[:, D_MODEL:2 * D_MODEL], w[:, 3 * D_MODEL:]],
                                axis=1).astype(BF16)
            q, k, vt, gate = _inproj_a(xf, row2(norm_g[i]), w, wvt)
            to3 = lambda t: t.reshape(bsz, seq, D_MODEL)
            slabs, slab_ids = _natten_bias(a_rpb[j] * math.log2(math.e))
            a = _natten(to3(q), to3(k), vt, to3(gate), slabs, slab_ids).reshape(m, D_MODEL)
            wo = a_w_out[j].astype(BF16)
        else:
            w = b_w_in[j]
            qk_w = D_MODEL + KV_W_B
            wvt = w[:, qk_w:qk_w + KV_W_B].T.astype(BF16)
            w = jnp.concatenate([_pair_heads(w[:, :D_MODEL], 1), w[:, D_MODEL:qk_w],
                                 _pair_heads(w[:, qk_w + KV_W_B:], 1)], axis=1).astype(BF16)
            head_gain = jnp.concatenate([jnp.tile(b_q_norm[j] * scale, HEADS), jnp.tile(b_k_norm[j], KV_HEADS_B)])
            q, k, vt, gate = _inproj_b(xf, row2(norm_g[i]), w, wvt, row2(head_gain), cos, sin, seg)
            score_bound = HEAD_DIM * jnp.max(jnp.abs(b_q_norm[j] * scale)) * jnp.max(jnp.abs(b_k_norm[j]))
            k3 = k.reshape(bsz, seq, KV_W_B)
            a = lax.cond(score_bound <= SCORE_BOUND,
                         lambda: _gqa(q, k3, vt, gate, bounded=True),
                         lambda: _gqa(q, k3, vt, gate, bounded=False))
            wo = _pair_heads(b_w_out[j], 0).astype(BF16)
        xf = _outproj(a, xf, p_all, i, wo, row2(ple_norm_g[i]), ple_w_gate[i].astype(BF16),
                      ple_w_proj[i].astype(BF16), row2(final_norm_g), final_norm=(i == depth - 1))
    return xf.reshape(bsz, seq, d)
```

```python
import functools
import math

import jax
import jax.numpy as jnp
import numpy as np
from jax import lax
from jax.experimental import pallas as pl
from jax.experimental.pallas import tpu as pltpu

D_MODEL = 1024
SEQ = 4096
GRID_W = 64
GRID_ROWS = SEQ // GRID_W
HEAD_DIM = 64
HEADS = D_MODEL // HEAD_DIM
KV_HEADS_B = 4
KV_W_B = KV_HEADS_B * HEAD_DIM
WIN_ROWS = 8
WIN_COLS = 16
ROPE_THETA = 10000.0
ROPE_SECTION = HEAD_DIM // 2
PLE_DIM = 256
NORM_EPS = 1e-6
NUM_MIXERS = 2

LANES = 128
HEAD_PAIRS = HEADS // 2
NEG_BIAS = -1e30
ONES_ROWS = 16
V_ROWS = LANES + ONES_ROWS

ROW_TILE = 512
A_QROWS = 4
A_KROWS = 12
A_KCHUNK = 4
A_TQ = A_QROWS * GRID_W
A_TK = A_KROWS * GRID_W
A_TKC = A_KCHUNK * GRID_W
A_NKC = A_KROWS // A_KCHUNK
A_BATCH = 2
A_SLOTS, A_AHEAD = 4, 2
B_SLOTS, B_AHEAD = 2, 1
B_UNROLL = 4
SCORE_BOUND = 50.0
B_TQ = 256
B_QGROUP = 2048
B_TKC = 4096
VMEM_LIMIT = 56 * 1024 * 1024

BF16 = jnp.bfloat16
F32 = jnp.float32


def _rms(x, g):
    ms = jnp.mean(x * x, axis=-1, keepdims=True)
    return x * lax.rsqrt(ms + NORM_EPS) * g


def _dot(a, b):
    return jnp.dot(a, b, preferred_element_type=F32)


def _dot_nt(a, b):
    return lax.dot_general(a, b, (((1,), (1,)), ((), ())), preferred_element_type=F32)


def _head_masks():
    lo = (lax.broadcasted_iota(jnp.int32, (1, LANES), 1) < HEAD_DIM).astype(BF16)
    return lo, 1 - lo


def _silu(g):
    return g * jax.nn.sigmoid(g)


def _score_steps(st_ref, m_ref, slot, key_chunk, n_chunks, chunk_scores):
    maxes = ([], [])

    def chunk(c):
        keys = pl.ds(c * key_chunk, key_chunk)
        for hh in range(2):
            st = chunk_scores(hh, keys)
            st_ref[slot, hh, keys, :] = st
            maxes[hh].append(jnp.max(st, axis=0, keepdims=True))

    def done():
        for hh in range(2):
            m_ref[slot, hh] = functools.reduce(jnp.maximum, maxes[hh])

    return [functools.partial(chunk, c) for c in range(n_chunks)] + [done]


def _normalise(accs):
    return jnp.concatenate([a[hh * HEAD_DIM:(hh + 1) * HEAD_DIM] / a[LANES:LANES + 1]
                            for hh, a in enumerate(accs)], axis=0)


def _softmax_pv_steps(st_ref, m_ref, slot, key_chunk, n_chunks, vt_chunk, emit):
    accs = [None, None]

    def chunk(c):
        keys = pl.ds(c * key_chunk, key_chunk)
        for hh in range(2):
            pt = jnp.exp2(st_ref[slot, hh, keys, :] - m_ref[slot, hh]).astype(BF16)
            pv = _dot(vt_chunk(c), pt)
            accs[hh] = pv if accs[hh] is None else accs[hh] + pv

    def done():
        emit(_normalise(accs))

    return [functools.partial(chunk, c) for c in range(n_chunks)] + [done]


def _pipeline(units, slots, ahead, scores, finish):
    assert units % slots == 0 and 0 < ahead < slots

    def run(*step_lists):
        for steps in zip(*step_lists):
            for step in reversed(steps):
                step()

    def trip(base, last):
        for k in range(slots):
            if not last or k + ahead < slots:
                run(scores(base + k + ahead, (k + ahead) % slots), finish(base + k, k))
            else:
                run(finish(base + k, k))

    for n in range(ahead):
        run(scores(n, n))

    def body(t, carry):
        trip(t * slots, False)
        return carry

    lax.fori_loop(0, units // slots - 1, body, 0)
    trip(units - slots, True)


def _inproj_a_kernel(x_ref, g_ref, w_ref, wvt_ref, q_ref, k_ref, vt_ref, gate_ref):
    h = _rms(x_ref[...], g_ref[...]).astype(BF16)
    for c, o_ref in enumerate((q_ref, k_ref, gate_ref)):
        o_ref[...] = _dot(h, w_ref[:, c * D_MODEL:(c + 1) * D_MODEL]).astype(BF16)
    vt = _dot_nt(wvt_ref[...], h).astype(BF16)
    for c in range(vt_ref.shape[1]):
        vt_ref[0, c] = vt[:, c * A_TKC:(c + 1) * A_TKC]


def _inproj_a(x, g, w, wvt):
    m = x.shape[0]
    out = jax.ShapeDtypeStruct((m, D_MODEL), BF16)
    tiles = SEQ // ROW_TILE
    fixed = lambda i: (0, 0)
    row_spec = pl.BlockSpec((ROW_TILE, D_MODEL), lambda i: (i, 0))
    return pl.pallas_call(
        _inproj_a_kernel,
        grid=(m // ROW_TILE,),
        in_specs=[row_spec,
                  pl.BlockSpec((1, D_MODEL), fixed),
                  pl.BlockSpec((D_MODEL, 3 * D_MODEL), fixed),
                  pl.BlockSpec((D_MODEL, D_MODEL), fixed)],
        out_specs=[row_spec, row_spec,
                   pl.BlockSpec((1, ROW_TILE // A_TKC, D_MODEL, A_TKC), lambda i: (i // tiles, i % tiles, 0, 0)),
                   row_spec],
        out_shape=[out, out, jax.ShapeDtypeStruct((m // SEQ, SEQ // A_TKC, D_MODEL, A_TKC), BF16), out],
        compiler_params=pltpu.CompilerParams(dimension_semantics=("parallel",), vmem_limit_bytes=VMEM_LIMIT),
        name="inproj_a",
    )(x, g, w, wvt)


def _inproj_b_kernel(x_ref, g_ref, w_ref, wvt_ref, hg_ref, cos_ref, sin_ref, seg_ref, q_ref, k_ref, vt_ref,
                     gate_ref):
    qk_w = D_MODEL + KV_W_B
    h = _rms(x_ref[...], g_ref[...]).astype(BF16)
    seg = seg_ref[...]
    cos = cos_ref[...]
    sin = sin_ref[...]
    first_half = (lax.broadcasted_iota(jnp.int32, (1, LANES), 1) % ROPE_SECTION) < (ROPE_SECTION // 2)
    yqk = _dot(h, w_ref[:, :qk_w])
    wide = 2 * LANES
    for pb in range(qk_w // wide):
        y = yqk[:, pb * wide:(pb + 1) * wide]
        ss = y * y
        hi = ss.astype(BF16)
        lo = (ss - hi.astype(F32)).astype(BF16)
        ms = _dot(jnp.concatenate([hi, lo], axis=1), seg) * (1.0 / HEAD_DIM)
        yn2 = y * lax.rsqrt(ms + NORM_EPS) * hg_ref[:, pb * wide:(pb + 1) * wide]
        for half in range(2):
            c = 2 * pb + half
            yn = yn2[:, half * LANES:(half + 1) * LANES]
            partner = jnp.where(first_half, pltpu.roll(yn, LANES - ROPE_SECTION // 2, 1),
                                pltpu.roll(yn, ROPE_SECTION // 2, 1))
            r = (yn * cos + partner * sin).astype(BF16)
            if c < D_MODEL // LANES:
                q_ref[c] = r
            else:
                k_ref[:, (c - D_MODEL // LANES) * LANES:(c - D_MODEL // LANES + 1) * LANES] = r
    vt_ref[0] = _dot_nt(wvt_ref[...], h).astype(BF16)
    gate = _dot(h, w_ref[:, qk_w:])
    for c in range(D_MODEL // LANES):
        gate_ref[c] = gate[:, c * LANES:(c + 1) * LANES].astype(BF16)


def _inproj_b(x, g, w, wvt, head_gain, cos, sin, seg):
    m = x.shape[0]
    n_w = 2 * D_MODEL + KV_W_B
    tiles = SEQ // ROW_TILE
    row = lambda i: (i, 0)
    fixed = lambda i: (0, 0)
    pos = lambda i: (i % tiles, 0)
    wide = jax.ShapeDtypeStruct((D_MODEL // LANES, m, LANES), BF16)
    wide_spec = pl.BlockSpec((D_MODEL // LANES, ROW_TILE, LANES), lambda i: (0, i, 0))
    return pl.pallas_call(
        _inproj_b_kernel,
        grid=(m // ROW_TILE,),
        in_specs=[pl.BlockSpec((ROW_TILE, D_MODEL), row),
                  pl.BlockSpec((1, D_MODEL), fixed),
                  pl.BlockSpec((D_MODEL, n_w), fixed),
                  pl.BlockSpec((KV_W_B, D_MODEL), fixed),
                  pl.BlockSpec((1, D_MODEL + KV_W_B), fixed),
                  pl.BlockSpec((ROW_TILE, LANES), pos),
                  pl.BlockSpec((ROW_TILE, LANES), pos),
                  pl.BlockSpec((4 * LANES, 2 * LANES), fixed)],
        out_specs=[wide_spec,
                   pl.BlockSpec((ROW_TILE, KV_W_B), row),
                   pl.BlockSpec((1, KV_W_B, ROW_TILE), lambda i: (i // tiles, 0, i % tiles)),
                   wide_spec],
        out_shape=[wide, jax.ShapeDtypeStruct((m, KV_W_B), BF16),
                   jax.ShapeDtypeStruct((m // SEQ, KV_W_B, SEQ), BF16), wide],
        compiler_params=pltpu.CompilerParams(dimension_semantics=("parallel",), vmem_limit_bytes=VMEM_LIMIT),
        name="inproj_b",
    )(x, g, w, wvt, head_gain, cos, sin, seg)


def _outproj_kernel(a_ref, x_ref, p_ref, wo_ref, pg_ref, wg_ref, wp_ref, fg_ref, o_ref, *, final_norm):
    if len(a_ref.shape) == 3:
        a = jnp.concatenate([a_ref[c] for c in range(a_ref.shape[0])], axis=1)
    else:
        a = a_ref[...]
    x1 = x_ref[...] + _dot(a, wo_ref[...])
    hn = _rms(x1, pg_ref[...]).astype(BF16)
    gate = jax.nn.sigmoid(_dot(hn, wg_ref[...]))
    x2 = x1 + gate * _dot(p_ref[...].astype(BF16), wp_ref[...])
    o_ref[...] = _rms(x2, fg_ref[...]) if final_norm else x2


def _outproj(a, x, p_all, layer, wo, pg, wg, wp, fg, final_norm):
    m = x.shape[0]
    row = lambda i: (i, 0)
    fixed = lambda i: (0, 0)
    return pl.pallas_call(
        functools.partial(_outproj_kernel, final_norm=final_norm),
        grid=(m // ROW_TILE,),
        in_specs=[pl.BlockSpec((ROW_TILE, D_MODEL), row) if a.ndim == 2 else
                  pl.BlockSpec((a.shape[0], ROW_TILE, LANES), lambda i: (0, i, 0)),
                  pl.BlockSpec((ROW_TILE, D_MODEL), row),
                  pl.BlockSpec((None, ROW_TILE, PLE_DIM), lambda i: (layer, i, 0)),
                  pl.BlockSpec((D_MODEL, D_MODEL), fixed),
                  pl.BlockSpec((1, D_MODEL), fixed),
                  pl.BlockSpec((D_MODEL, D_MODEL), fixed),
                  pl.BlockSpec((PLE_DIM, D_MODEL), fixed),
                  pl.BlockSpec((1, D_MODEL), fixed)],
        out_specs=pl.BlockSpec((ROW_TILE, D_MODEL), row),
        out_shape=jax.ShapeDtypeStruct((m, D_MODEL), F32),
        compiler_params=pltpu.CompilerParams(dimension_semantics=("parallel",), vmem_limit_bytes=VMEM_LIMIT),
        name="outproj_final" if final_norm else "outproj",
    )(a, x, p_all, wo, pg, wg, wp, fg)


def _natten_kernel(slab_ids_ref, q_ref, k_ref, vt_ref, gate_ref, bias_ref, o_ref, st_ref, m_ref):
    n_blocks = GRID_ROWS // A_QROWS
    masks = _head_masks()
    ones = jnp.ones((ONES_ROWS, A_TK), BF16)

    def where(n):
        n = jnp.asarray(n, jnp.int32)
        bb, i = n // n_blocks, n % n_blocks
        first_chunk = jnp.clip(i - 1, 0, n_blocks - A_NKC)
        variant = jnp.where(i == 0, 0, jnp.where(i == n_blocks - 1, 2, 1))
        return bb, pl.ds(pl.multiple_of(i * A_TQ, A_TQ), A_TQ), first_chunk, variant

    def scores(n, slot):
        bb, q_rows, first_chunk, variant = where(n)
        q2 = q_ref[bb, q_rows, :]
        qm = [q2 * masks[hh] for hh in range(2)]

        def chunk_scores(hh, keys):
            k_rows = pl.ds(pl.multiple_of(first_chunk * A_TKC, A_TKC), A_TK)
            bias = jnp.concatenate(
                [jnp.concatenate([bias_ref[hh, slab_ids_ref[(variant * A_KROWS + r) * 2 + half]]
                                  for half in range(2)], axis=1) for r in range(A_KROWS)], axis=0)
            return _dot_nt(k_ref[bb, k_rows, :], qm[hh]) + bias

        return _score_steps(st_ref, m_ref, slot, A_TK, 1, chunk_scores)

    def finish(n, slot):
        bb, q_rows, first_chunk, _ = where(n)

        def vt_chunk(c):
            vt = jnp.concatenate([vt_ref[bb, first_chunk + kc] for kc in range(A_NKC)], axis=1)
            return jnp.concatenate([vt, ones], axis=0)

        def emit(ot):
            o_ref[bb, q_rows, :] = (ot.T * _silu(gate_ref[bb, q_rows, :].astype(F32))).astype(BF16)

        return _softmax_pv_steps(st_ref, m_ref, slot, A_TK, 1, vt_chunk, emit)

    _pipeline(q_ref.shape[0] * n_blocks, st_ref.shape[0], A_AHEAD, scores, finish)


def _natten(q, k, vt, gate, slabs, slab_ids):
    bsz = q.shape[0]
    tok_spec = pl.BlockSpec((A_BATCH, SEQ, LANES), lambda p, g: (g, 0, p))
    return pl.pallas_call(
        _natten_kernel,
        grid=(HEAD_PAIRS, bsz // A_BATCH),
        in_specs=[pl.BlockSpec(memory_space=pltpu.SMEM),
                  tok_spec, tok_spec,
                  pl.BlockSpec((A_BATCH, SEQ // A_TKC, LANES, A_TKC), lambda p, g: (g, 0, p, 0)),
                  tok_spec,
                  pl.BlockSpec((2,) + slabs.shape[1:], lambda p, g: (p, 0, 0, 0))],
        out_specs=tok_spec,
        out_shape=jax.ShapeDtypeStruct(q.shape, BF16),
        scratch_shapes=[pltpu.VMEM((A_SLOTS, 2, A_TK, A_TQ), F32), pltpu.VMEM((A_SLOTS, 2, 1, A_TQ), F32)],
        compiler_params=pltpu.CompilerParams(dimension_semantics=("parallel", "parallel"),
                                             vmem_limit_bytes=VMEM_LIMIT),
        name="natten",
    )(slab_ids, q, k, vt, gate, slabs)


def _natten_bias(rpb):
    qc = np.arange(GRID_W)
    kc = np.arange(GRID_W)
    cs = np.clip(qc - WIN_COLS // 2, 0, GRID_W - WIN_COLS)
    col_ok = (kc[:, None] >= cs[None, :]) & (kc[:, None] < cs[None, :] + WIN_COLS)
    dc = np.clip(kc[:, None] - qc[None, :] + WIN_COLS - 1, 0, 2 * WIN_COLS - 2)
    table = jnp.where(col_ok[None, None], jnp.take(rpb, jnp.asarray(dc), axis=2), NEG_BIAS)
    n_rel = 2 * WIN_ROWS - 1
    table = jnp.concatenate([table, jnp.full((HEADS, 1, GRID_W, GRID_W), NEG_BIAS, F32)], axis=1)
    n_blocks = GRID_ROWS // A_QROWS
    pairs, ids = [], []
    for i in (0, 1, n_blocks - 1):
        ws = A_KCHUNK * min(max(i - 1, 0), n_blocks - A_NKC)
        for b in range(A_KROWS):
            for half in range(A_QROWS // 2):
                rel = []
                for a in (2 * half, 2 * half + 1):
                    qr = A_QROWS * i + a
                    rs = min(max(qr - WIN_ROWS // 2, 0), GRID_ROWS - WIN_ROWS)
                    rel.append(ws + b - qr + WIN_ROWS - 1 if rs <= ws + b < rs + WIN_ROWS else n_rel)
                if tuple(rel) not in pairs:
                    pairs.append(tuple(rel))
                ids.append(pairs.index(tuple(rel)))
    left, right = (np.asarray([pr[side] for pr in pairs], np.int32) for side in range(2))
    slabs = jnp.concatenate([jnp.take(table, left, axis=1), jnp.take(table, right, axis=1)], axis=-1)
    return slabs, jnp.asarray(ids, jnp.int32)


def _gqa_kernel(q_ref, k_ref, vt_ref, gate_ref, o_ref, vt_ext_ref, *scratch, bounded):
    blocks, qgroup, _ = q_ref.shape
    units = blocks * (qgroup // B_TQ)
    n_chunks = SEQ // B_TKC

    @pl.when(pl.program_id(2) == 0)
    def _():
        vt_ext_ref[:LANES] = vt_ref[0]
        vt_ext_ref[LANES:] = jnp.ones((ONES_ROWS, SEQ), BF16)

    masks = _head_masks()

    def rows(n):
        start = (n // blocks) * B_TQ
        return pl.ds(start if isinstance(n, int) else pl.multiple_of(start, B_TQ), B_TQ)

    def emit(n, ot):
        g = gate_ref[n % blocks, rows(n), :].astype(F32)
        o_ref[n % blocks, rows(n), :] = (ot.T * _silu(g)).astype(BF16)

    def masked_q(n):
        q2 = q_ref[n % blocks, rows(n), :]
        return [q2 * masks[hh] for hh in range(2)]

    def vt_chunk(c):
        return vt_ext_ref[:, c * B_TKC:(c + 1) * B_TKC]

    if bounded:
        def body(t, carry):
            items = [(t * B_UNROLL + u, c) for u in range(B_UNROLL) for c in range(n_chunks)]
            qm, st, accs = {}, {}, {}

            def qk(idx):
                n, c = items[idx]
                if c == 0:
                    qm[idx // n_chunks] = masked_q(n)
                for hh in range(2):
                    st[idx, hh] = _dot_nt(k_ref[0, c * B_TKC:(c + 1) * B_TKC, :], qm[idx // n_chunks][hh])

            qk(0)
            for idx, (n, c) in enumerate(items):
                if idx + 1 < len(items):
                    qk(idx + 1)
                for hh in range(2):
                    pv = _dot(vt_chunk(c), jnp.exp2(st.pop((idx, hh))).astype(BF16))
                    accs[hh] = pv if c == 0 else accs[hh] + pv
                if c == n_chunks - 1:
                    emit(n, _normalise([accs[0], accs[1]]))
            return carry

        lax.fori_loop(0, units // B_UNROLL, body, 0)
        return

    st_ref, m_ref = scratch

    def scores(n, slot):
        qm = masked_q(n)
        return _score_steps(st_ref, m_ref, slot, B_TKC, n_chunks,
                            lambda hh, keys: _dot_nt(k_ref[0, keys, :], qm[hh]))

    def finish(n, slot):
        return _softmax_pv_steps(st_ref, m_ref, slot, B_TKC, n_chunks, vt_chunk, functools.partial(emit, n))

    _pipeline(units, st_ref.shape[0], B_AHEAD, scores, finish)


def _gqa(q, k, vt, gate, bounded):
    bsz = k.shape[0]
    blocks_per_pair = HEAD_PAIRS // (KV_HEADS_B // 2)
    groups = SEQ // B_QGROUP
    q_spec = pl.BlockSpec((blocks_per_pair, B_QGROUP, LANES), lambda b, j, g: (j, b * groups + g, 0))
    scratch = [pltpu.VMEM((V_ROWS, SEQ), BF16)]
    if not bounded:
        scratch += [pltpu.VMEM((B_SLOTS, 2, SEQ, B_TQ), F32), pltpu.VMEM((B_SLOTS, 2, 1, B_TQ), F32)]
    return pl.pallas_call(
        functools.partial(_gqa_kernel, bounded=bounded),
        grid=(bsz, KV_HEADS_B // 2, groups),
        in_specs=[q_spec,
                  pl.BlockSpec((1, SEQ, LANES), lambda b, j, g: (b, 0, j)),
                  pl.BlockSpec((1, LANES, SEQ), lambda b, j, g: (b, j, 0)),
                  q_spec],
        out_specs=q_spec,
        out_shape=jax.ShapeDtypeStruct(q.shape, BF16),
        scratch_shapes=scratch,
        compiler_params=pltpu.CompilerParams(
            dimension_semantics=("parallel", "parallel", "arbitrary"),
            vmem_limit_bytes=VMEM_LIMIT),
        name="gqa_bounded" if bounded else "gqa",
    )(q, k, vt, gate)


def _pair_heads(w, axis):
    lead, trail = w.shape[:axis], w.shape[axis + 1:]
    group = HEADS // KV_HEADS_B
    w = w.reshape(*lead, KV_HEADS_B // 2, 2, group, HEAD_DIM, *trail)
    w = jnp.swapaxes(w, len(lead) + 1, len(lead) + 2)
    return w.reshape(*lead, D_MODEL, *trail)


def _rope_tables():
    t = np.arange(SEQ)
    pos = np.stack([t // GRID_W, t % GRID_W], axis=1).astype(np.float32)
    lane = np.arange(LANES) % HEAD_DIM
    section = lane // ROPE_SECTION
    n_freq = ROPE_SECTION // 2
    inv = jnp.power(ROPE_THETA, -jnp.arange(n_freq, dtype=F32) * 2.0 / ROPE_SECTION)
    ang = jnp.asarray(pos)[:, section] * inv[lane % n_freq][None, :]
    sign = np.where(lane % ROPE_SECTION < n_freq, -1.0, 1.0).astype(np.float32)
    return jnp.cos(ang), jnp.sin(ang) * sign[None, :]


def kernel(x, p, norm_g, a_w_in, a_rpb, a_w_out, b_w_in, b_q_norm, b_k_norm, b_w_out, ple_norm_g, ple_w_gate,
           ple_w_proj, final_norm_g):
    bsz, seq, d = x.shape
    assert (seq, d) == (SEQ, D_MODEL)
    depth = p.shape[0]
    m = bsz * seq
    scale = math.log2(math.e) / math.sqrt(HEAD_DIM)
    xf = x.reshape(m, d)
    p_all = p.reshape(depth, m, PLE_DIM)
    cos, sin = _rope_tables()
    seg = np.kron(np.eye(4), np.ones((HEAD_DIM, HEAD_DIM)))
    seg = jnp.asarray(np.concatenate([seg, seg], axis=0), BF16)
    row2 = lambda v: v.reshape(1, -1)

    for i in range(depth):
        j = i // NUM_MIXERS
        if i % NUM_MIXERS == 0:
            w = a_w_in[j]
            wvt = w[:, 2 * D_MODEL:3 * D_MODEL].T.astype(BF16)
            w = jnp.concatenate([w[:, :D_MODEL] * scale, w[:, D_MODEL:2 * D_MODEL], w[:, 3 * D_MODEL:]],
                                axis=1).astype(BF16)
            q, k, vt, gate = _inproj_a(xf, row2(norm_g[i]), w, wvt)
            to3 = lambda t: t.reshape(bsz, seq, D_MODEL)
            slabs, slab_ids = _natten_bias(a_rpb[j] * math.log2(math.e))
            a = _natten(to3(q), to3(k), vt, to3(gate), slabs, slab_ids).reshape(m, D_MODEL)
            wo = a_w_out[j].astype(BF16)
        else:
            w = b_w_in[j]
            qk_w = D_MODEL + KV_W_B
            wvt = w[:, qk_w:qk_w + KV_W_B].T.astype(BF16)
            w = jnp.concatenate([_pair_heads(w[:, :D_MODEL], 1), w[:, D_MODEL:qk_w],
                                 _pair_heads(w[:, qk_w + KV_W_B:], 1)], axis=1).astype(BF16)
            head_gain = jnp.concatenate([jnp.tile(b_q_norm[j] * scale, HEADS), jnp.tile(b_k_norm[j], KV_HEADS_B)])
            q, k, vt, gate = _inproj_b(xf, row2(norm_g[i]), w, wvt, row2(head_gain), cos, sin, seg)
            score_bound = HEAD_DIM * jnp.max(jnp.abs(b_q_norm[j] * scale)) * jnp.max(jnp.abs(b_k_norm[j]))
            k3 = k.reshape(bsz, seq, KV_W_B)
            a = lax.cond(score_bound <= SCORE_BOUND,
                         lambda: _gqa(q, k3, vt, gate, bounded=True),
                         lambda: _gqa(q, k3, vt, gate, bounded=False))
            wo = _pair_heads(b_w_out[j], 0).astype(BF16)
        xf = _outproj(a, xf, p_all, i, wo, row2(ple_norm_g[i]), ple_w_gate[i].astype(BF16),
                      ple_w_proj[i].astype(BF16), row2(final_norm_g), final_norm=(i == depth - 1))
    return xf.reshape(bsz, seq, d)
```

```python
import functools
import math

import jax
import jax.numpy as jnp
import numpy as np
from jax import lax
from jax.experimental import pallas as pl
from jax.experimental.pallas import tpu as pltpu

D_MODEL = 1024
SEQ = 4096
GRID_W = 64
GRID_ROWS = SEQ // GRID_W
HEAD_DIM = 64
HEADS = D_MODEL // HEAD_DIM
KV_HEADS_B = 4
KV_W_B = KV_HEADS_B * HEAD_DIM
WIN_ROWS = 8
WIN_COLS = 16
ROPE_THETA = 10000.0
ROPE_SECTION = HEAD_DIM // 2
PLE_DIM = 256
NORM_EPS = 1e-6
NUM_MIXERS = 2

LANES = 128
HEAD_PAIRS = HEADS // 2
NEG_BIAS = -1e30
ONES_ROWS = 16
V_ROWS = LANES + ONES_ROWS

ROW_TILE = 512
A_QROWS = 4
A_KROWS = 12
A_KCHUNK = 4
A_TQ = A_QROWS * GRID_W
A_TK = A_KROWS * GRID_W
A_TKC = A_KCHUNK * GRID_W
A_NKC = A_KROWS // A_KCHUNK
A_BATCH = 2
A_UNROLL = 4
A_SLOTS, A_AHEAD = 4, 2
B_SLOTS, B_AHEAD = 2, 1
B_UNROLL = 4
SCORE_BOUND = 50.0
B_TQ = 256
B_QGROUP = 2048
B_TKC = 4096
VMEM_LIMIT = 56 * 1024 * 1024

BF16 = jnp.bfloat16
F32 = jnp.float32


def _rms(x, g):
    ms = jnp.mean(x * x, axis=-1, keepdims=True)
    return x * lax.rsqrt(ms + NORM_EPS) * g


def _dot(a, b):
    return jnp.dot(a, b, preferred_element_type=F32)


def _dot_nt(a, b):
    return lax.dot_general(a, b, (((1,), (1,)), ((), ())), preferred_element_type=F32)


def _head_masks():
    lo = (lax.broadcasted_iota(jnp.int32, (1, LANES), 1) < HEAD_DIM).astype(BF16)
    return lo, 1 - lo


def _silu(g):
    return g * jax.nn.sigmoid(g)


def _score_steps(st_ref, m_ref, slot, key_chunk, n_chunks, chunk_scores):
    maxes = ([], [])

    def chunk(c):
        keys = pl.ds(c * key_chunk, key_chunk)
        for hh in range(2):
            st = chunk_scores(hh, keys)
            st_ref[slot, hh, keys, :] = st
            maxes[hh].append(jnp.max(st, axis=0, keepdims=True))

    def done():
        for hh in range(2):
            m_ref[slot, hh] = functools.reduce(jnp.maximum, maxes[hh])

    return [functools.partial(chunk, c) for c in range(n_chunks)] + [done]


def _normalise(accs):
    return jnp.concatenate([a[hh * HEAD_DIM:(hh + 1) * HEAD_DIM] / a[LANES:LANES + 1]
                            for hh, a in enumerate(accs)], axis=0)


def _softmax_pv_steps(st_ref, m_ref, slot, key_chunk, n_chunks, vt_chunk, emit):
    accs = [None, None]

    def chunk(c):
        keys = pl.ds(c * key_chunk, key_chunk)
        for hh in range(2):
            pt = jnp.exp2(st_ref[slot, hh, keys, :] - m_ref[slot, hh]).astype(BF16)
            pv = _dot(vt_chunk(c), pt)
            accs[hh] = pv if accs[hh] is None else accs[hh] + pv

    def done():
        emit(_normalise(accs))

    return [functools.partial(chunk, c) for c in range(n_chunks)] + [done]


def _pipeline(units, slots, ahead, scores, finish):
    assert units % slots == 0 and 0 < ahead < slots

    def run(*step_lists):
        for steps in zip(*step_lists):
            for step in reversed(steps):
                step()

    def trip(base, last):
        for k in range(slots):
            if not last or k + ahead < slots:
                run(scores(base + k + ahead, (k + ahead) % slots), finish(base + k, k))
            else:
                run(finish(base + k, k))

    for n in range(ahead):
        run(scores(n, n))

    def body(t, carry):
        trip(t * slots, False)
        return carry

    lax.fori_loop(0, units // slots - 1, body, 0)
    trip(units - slots, True)


def _inproj_a_kernel(x_ref, g_ref, w_ref, seg_ref, q_ref, k_ref, vt_ref, gate_ref, n2_ref):
    h = _rms(x_ref[...], g_ref[...]).astype(BF16)
    wide = 2 * LANES
    head_norms = []
    for c, o_ref in enumerate((q_ref, k_ref)):
        y = _dot(h, w_ref[:, c * D_MODEL:(c + 1) * D_MODEL])
        o_ref[...] = y.astype(BF16)
        ss = (y * y).astype(BF16)
        n2 = jnp.concatenate([_dot(ss[:, b * wide:(b + 1) * wide], seg_ref[...]) for b in range(D_MODEL // wide)],
                             axis=1)
        head_norms.append(jnp.max(n2, axis=0, keepdims=True))
    n2_ref[0] = jnp.concatenate(head_norms, axis=0)
    gate_ref[...] = _dot(h, w_ref[:, 3 * D_MODEL:]).astype(BF16)
    vt = lax.dot_general(w_ref[:, 2 * D_MODEL:3 * D_MODEL], h, (((0,), (1,)), ((), ())),
                         preferred_element_type=F32).astype(BF16)
    for c in range(vt_ref.shape[1]):
        vt_ref[0, c] = vt[:, c * A_TKC:(c + 1) * A_TKC]


def _inproj_a(x, g, w, seg):
    m = x.shape[0]
    out = jax.ShapeDtypeStruct((m, D_MODEL), BF16)
    tiles = SEQ // ROW_TILE
    fixed = lambda i: (0, 0)
    row_spec = pl.BlockSpec((ROW_TILE, D_MODEL), lambda i: (i, 0))
    return pl.pallas_call(
        _inproj_a_kernel,
        grid=(m // ROW_TILE,),
        in_specs=[row_spec,
                  pl.BlockSpec((1, D_MODEL), fixed),
                  pl.BlockSpec((D_MODEL, 4 * D_MODEL), fixed),
                  pl.BlockSpec((2 * LANES, 2 * LANES), fixed)],
        out_specs=[row_spec, row_spec,
                   pl.BlockSpec((1, ROW_TILE // A_TKC, D_MODEL, A_TKC), lambda i: (i // tiles, i % tiles, 0, 0)),
                   row_spec,
                   pl.BlockSpec((1, 2, D_MODEL), lambda i: (i, 0, 0))],
        out_shape=[out, out, jax.ShapeDtypeStruct((m // SEQ, SEQ // A_TKC, D_MODEL, A_TKC), BF16), out,
                   jax.ShapeDtypeStruct((m // ROW_TILE, 2, D_MODEL), F32)],
        compiler_params=pltpu.CompilerParams(dimension_semantics=("parallel",), vmem_limit_bytes=VMEM_LIMIT),
        name="inproj_a",
    )(x, g, w, seg)


def _inproj_b_kernel(x_ref, g_ref, w_ref, hg_ref, cos_ref, sin_ref, seg_ref, q_ref, k_ref, vt_ref, gate_ref):
    qk_w = D_MODEL + KV_W_B
    h = _rms(x_ref[...], g_ref[...]).astype(BF16)
    seg = seg_ref[...]
    cos = cos_ref[...]
    sin = sin_ref[...]
    first_half = (lax.broadcasted_iota(jnp.int32, (1, LANES), 1) % ROPE_SECTION) < (ROPE_SECTION // 2)
    yqk = _dot(h, w_ref[:, :qk_w])
    wide = 2 * LANES
    for pb in range(qk_w // wide):
        y = yqk[:, pb * wide:(pb + 1) * wide]
        ss = y * y
        hi = ss.astype(BF16)
        lo = (ss - hi.astype(F32)).astype(BF16)
        ms = _dot(jnp.concatenate([hi, lo], axis=1), seg) * (1.0 / HEAD_DIM)
        yn2 = y * lax.rsqrt(ms + NORM_EPS) * hg_ref[:, pb * wide:(pb + 1) * wide]
        for half in range(2):
            c = 2 * pb + half
            yn = yn2[:, half * LANES:(half + 1) * LANES]
            partner = jnp.where(first_half, pltpu.roll(yn, LANES - ROPE_SECTION // 2, 1),
                                pltpu.roll(yn, ROPE_SECTION // 2, 1))
            r = (yn * cos + partner * sin).astype(BF16)
            if c < D_MODEL // LANES:
                q_ref[c] = r
            else:
                k_ref[:, (c - D_MODEL // LANES) * LANES:(c - D_MODEL // LANES + 1) * LANES] = r
    vt_ref[0] = lax.dot_general(w_ref[:, qk_w:qk_w + KV_W_B], h, (((0,), (1,)), ((), ())),
                                preferred_element_type=F32).astype(BF16)
    gate = _dot(h, w_ref[:, qk_w + KV_W_B:])
    for c in range(D_MODEL // LANES):
        gate_ref[c] = gate[:, c * LANES:(c + 1) * LANES].astype(BF16)


def _inproj_b(x, g, w, head_gain, cos, sin, seg):
    m = x.shape[0]
    n_w = 2 * D_MODEL + 2 * KV_W_B
    tiles = SEQ // ROW_TILE
    row = lambda i: (i, 0)
    fixed = lambda i: (0, 0)
    pos = lambda i: (i % tiles, 0)
    wide = jax.ShapeDtypeStruct((D_MODEL // LANES, m, LANES), BF16)
    wide_spec = pl.BlockSpec((D_MODEL // LANES, ROW_TILE, LANES), lambda i: (0, i, 0))
    return pl.pallas_call(
        _inproj_b_kernel,
        grid=(m // ROW_TILE,),
        in_specs=[pl.BlockSpec((ROW_TILE, D_MODEL), row),
                  pl.BlockSpec((1, D_MODEL), fixed),
                  pl.BlockSpec((D_MODEL, n_w), fixed),
                  pl.BlockSpec((1, D_MODEL + KV_W_B), fixed),
                  pl.BlockSpec((ROW_TILE, LANES), pos),
                  pl.BlockSpec((ROW_TILE, LANES), pos),
                  pl.BlockSpec((4 * LANES, 2 * LANES), fixed)],
        out_specs=[wide_spec,
                   pl.BlockSpec((ROW_TILE, KV_W_B), row),
                   pl.BlockSpec((1, KV_W_B, ROW_TILE), lambda i: (i // tiles, 0, i % tiles)),
                   wide_spec],
        out_shape=[wide, jax.ShapeDtypeStruct((m, KV_W_B), BF16),
                   jax.ShapeDtypeStruct((m // SEQ, KV_W_B, SEQ), BF16), wide],
        compiler_params=pltpu.CompilerParams(dimension_semantics=("parallel",), vmem_limit_bytes=VMEM_LIMIT),
        name="inproj_b",
    )(x, g, w, head_gain, cos, sin, seg)


def _outproj_kernel(a_ref, x_ref, p_ref, wo_ref, pg_ref, wg_ref, wp_ref, fg_ref, o_ref, *, final_norm):
    if len(a_ref.shape) == 3:
        a = jnp.concatenate([a_ref[c] for c in range(a_ref.shape[0])], axis=1)
    else:
        a = a_ref[...]
    x1 = x_ref[...] + _dot(a, wo_ref[...])
    hn = _rms(x1, pg_ref[...]).astype(BF16)
    gate = jax.nn.sigmoid(_dot(hn, wg_ref[...]))
    x2 = x1 + gate * _dot(p_ref[...].astype(BF16), wp_ref[...])
    o_ref[...] = _rms(x2, fg_ref[...]) if final_norm else x2


def _outproj(a, x, p_all, layer, wo, pg, wg, wp, fg, final_norm):
    m = x.shape[0]
    row = lambda i: (i, 0)
    fixed = lambda i: (0, 0)
    return pl.pallas_call(
        functools.partial(_outproj_kernel, final_norm=final_norm),
        grid=(m // ROW_TILE,),
        in_specs=[pl.BlockSpec((ROW_TILE, D_MODEL), row) if a.ndim == 2 else
                  pl.BlockSpec((a.shape[0], ROW_TILE, LANES), lambda i: (0, i, 0)),
                  pl.BlockSpec((ROW_TILE, D_MODEL), row),
                  pl.BlockSpec((None, ROW_TILE, PLE_DIM), lambda i: (layer, i, 0)),
                  pl.BlockSpec((D_MODEL, D_MODEL), fixed),
                  pl.BlockSpec((1, D_MODEL), fixed),
                  pl.BlockSpec((D_MODEL, D_MODEL), fixed),
                  pl.BlockSpec((PLE_DIM, D_MODEL), fixed),
                  pl.BlockSpec((1, D_MODEL), fixed)],
        out_specs=pl.BlockSpec((ROW_TILE, D_MODEL), row),
        out_shape=jax.ShapeDtypeStruct((m, D_MODEL), F32),
        compiler_params=pltpu.CompilerParams(dimension_semantics=("parallel",), vmem_limit_bytes=VMEM_LIMIT),
        name="outproj_final" if final_norm else "outproj",
    )(a, x, p_all, wo, pg, wg, wp, fg)


def _natten_kernel(slab_ids_ref, q_ref, k_ref, vt_ref, gate_ref, bias_ref, o_ref, *scratch, bounded):
    n_blocks = GRID_ROWS // A_QROWS
    masks = _head_masks()
    ones = jnp.ones((ONES_ROWS, A_TK), BF16)

    def where(n):
        n = jnp.asarray(n, jnp.int32)
        bb, i = n // n_blocks, n % n_blocks
        first_chunk = jnp.clip(i - 1, 0, n_blocks - A_NKC)
        variant = jnp.where(i == 0, 0, jnp.where(i == n_blocks - 1, 2, 1))
        return bb, pl.ds(pl.multiple_of(i * A_TQ, A_TQ), A_TQ), first_chunk, variant

    def unit_scores(n):
        bb, q_rows, first_chunk, variant = where(n)
        q2 = q_ref[bb, q_rows, :]
        kwin = k_ref[bb, pl.ds(pl.multiple_of(first_chunk * A_TKC, A_TKC), A_TK), :]
        out = []
        for hh in range(2):
            bias = jnp.concatenate(
                [jnp.concatenate([bias_ref[hh, slab_ids_ref[(variant * A_KROWS + r) * 2 + half]]
                                  for half in range(2)], axis=1) for r in range(A_KROWS)], axis=0)
            out.append(_dot_nt(kwin, q2 * masks[hh]) + bias)
        return out

    def vt_ext(n):
        bb, _, first_chunk, _ = where(n)
        vt = jnp.concatenate([vt_ref[bb, first_chunk + kc] for kc in range(A_NKC)], axis=1)
        return jnp.concatenate([vt, ones], axis=0)

    def emit(n, ot):
        bb, q_rows, _, _ = where(n)
        o_ref[bb, q_rows, :] = (ot.T * _silu(gate_ref[bb, q_rows, :].astype(F32))).astype(BF16)

    units = q_ref.shape[0] * n_blocks
    if bounded:
        def body(t, carry):
            st = {0: unit_scores(t * A_UNROLL)}
            for u in range(A_UNROLL):
                n = t * A_UNROLL + u
                if u + 1 < A_UNROLL:
                    st[u + 1] = unit_scores(n + 1)
                vt = vt_ext(n)
                emit(n, _normalise([_dot(vt, jnp.exp2(sc).astype(BF16)) for sc in st.pop(u)]))
            return carry

        lax.fori_loop(0, units // A_UNROLL, body, 0)
        return

    st_ref, m_ref = scratch

    def scores(n, slot):
        sc = unit_scores(n)
        return _score_steps(st_ref, m_ref, slot, A_TK, 1, lambda hh, keys: sc[hh])

    def finish(n, slot):
        return _softmax_pv_steps(st_ref, m_ref, slot, A_TK, 1, lambda c: vt_ext(n), functools.partial(emit, n))

    _pipeline(units, st_ref.shape[0], A_AHEAD, scores, finish)


def _natten(q, k, vt, gate, slabs, slab_ids, bounded):
    bsz = q.shape[0]
    tok_spec = pl.BlockSpec((A_BATCH, SEQ, LANES), lambda p, g: (g, 0, p))
    scratch = [] if bounded else [pltpu.VMEM((A_SLOTS, 2, A_TK, A_TQ), F32),
                                  pltpu.VMEM((A_SLOTS, 2, 1, A_TQ), F32)]
    return pl.pallas_call(
        functools.partial(_natten_kernel, bounded=bounded),
        grid=(HEAD_PAIRS, bsz // A_BATCH),
        in_specs=[pl.BlockSpec(memory_space=pltpu.SMEM),
                  tok_spec, tok_spec,
                  pl.BlockSpec((A_BATCH, SEQ // A_TKC, LANES, A_TKC), lambda p, g: (g, 0, p, 0)),
                  tok_spec,
                  pl.BlockSpec((2,) + slabs.shape[1:], lambda p, g: (p, 0, 0, 0))],
        out_specs=tok_spec,
        out_shape=jax.ShapeDtypeStruct(q.shape, BF16),
        scratch_shapes=scratch,
        compiler_params=pltpu.CompilerParams(dimension_semantics=("parallel", "parallel"),
                                             vmem_limit_bytes=VMEM_LIMIT),
        name="natten_bounded" if bounded else "natten",
    )(slab_ids, q, k, vt, gate, slabs)


def _natten_bias(rpb):
    qc = np.arange(GRID_W)
    kc = np.arange(GRID_W)
    cs = np.clip(qc - WIN_COLS // 2, 0, GRID_W - WIN_COLS)
    col_ok = (kc[:, None] >= cs[None, :]) & (kc[:, None] < cs[None, :] + WIN_COLS)
    dc = np.clip(kc[:, None] - qc[None, :] + WIN_COLS - 1, 0, 2 * WIN_COLS - 2)
    table = jnp.where(col_ok[None, None], jnp.take(rpb, jnp.asarray(dc), axis=2), NEG_BIAS)
    n_rel = 2 * WIN_ROWS - 1
    table = jnp.concatenate([table, jnp.full((HEADS, 1, GRID_W, GRID_W), NEG_BIAS, F32)], axis=1)
    n_blocks = GRID_ROWS // A_QROWS
    pairs, ids = [], []
    for i in (0, 1, n_blocks - 1):
        ws = A_KCHUNK * min(max(i - 1, 0), n_blocks - A_NKC)
        for b in range(A_KROWS):
            for half in range(A_QROWS // 2):
                rel = []
                for a in (2 * half, 2 * half + 1):
                    qr = A_QROWS * i + a
                    rs = min(max(qr - WIN_ROWS // 2, 0), GRID_ROWS - WIN_ROWS)
                    rel.append(ws + b - qr + WIN_ROWS - 1 if rs <= ws + b < rs + WIN_ROWS else n_rel)
                if tuple(rel) not in pairs:
                    pairs.append(tuple(rel))
                ids.append(pairs.index(tuple(rel)))
    left, right = (np.asarray([pr[side] for pr in pairs], np.int32) for side in range(2))
    slabs = jnp.concatenate([jnp.take(table, left, axis=1), jnp.take(table, right, axis=1)], axis=-1)
    return slabs, jnp.asarray(ids, jnp.int32)


def _gqa_kernel(q_ref, k_ref, vt_ref, gate_ref, o_ref, vt_ext_ref, *scratch, bounded):
    blocks, qgroup, _ = q_ref.shape
    units = blocks * (qgroup // B_TQ)
    n_chunks = SEQ // B_TKC

    @pl.when(pl.program_id(2) == 0)
    def _():
        vt_ext_ref[:LANES] = vt_ref[0]
        vt_ext_ref[LANES:] = jnp.ones((ONES_ROWS, SEQ), BF16)

    masks = _head_masks()

    def rows(n):
        start = (n // blocks) * B_TQ
        return pl.ds(start if isinstance(n, int) else pl.multiple_of(start, B_TQ), B_TQ)

    def emit(n, ot):
        g = gate_ref[n % blocks, rows(n), :].astype(F32)
        o_ref[n % blocks, rows(n), :] = (ot.T * _silu(g)).astype(BF16)

    def masked_q(n):
        q2 = q_ref[n % blocks, rows(n), :]
        return [q2 * masks[hh] for hh in range(2)]

    def vt_chunk(c):
        return vt_ext_ref[:, c * B_TKC:(c + 1) * B_TKC]

    if bounded:
        def body(t, carry):
            items = [(t * B_UNROLL + u, c) for u in range(B_UNROLL) for c in range(n_chunks)]
            qm, st, accs = {}, {}, {}

            def qk(idx):
                n, c = items[idx]
                if c == 0:
                    qm[idx // n_chunks] = masked_q(n)
                for hh in range(2):
                    st[idx, hh] = _dot_nt(k_ref[0, c * B_TKC:(c + 1) * B_TKC, :], qm[idx // n_chunks][hh])

            qk(0)
            for idx, (n, c) in enumerate(items):
                if idx + 1 < len(items):
                    qk(idx + 1)
                for hh in range(2):
                    pv = _dot(vt_chunk(c), jnp.exp2(st.pop((idx, hh))).astype(BF16))
                    accs[hh] = pv if c == 0 else accs[hh] + pv
                if c == n_chunks - 1:
                    emit(n, _normalise([accs[0], accs[1]]))
            return carry

        lax.fori_loop(0, units // B_UNROLL, body, 0)
        return

    st_ref, m_ref = scratch

    def scores(n, slot):
        qm = masked_q(n)
        return _score_steps(st_ref, m_ref, slot, B_TKC, n_chunks,
                            lambda hh, keys: _dot_nt(k_ref[0, keys, :], qm[hh]))

    def finish(n, slot):
        return _softmax_pv_steps(st_ref, m_ref, slot, B_TKC, n_chunks, vt_chunk, functools.partial(emit, n))

    _pipeline(units, st_ref.shape[0], B_AHEAD, scores, finish)


def _gqa(q, k, vt, gate, bounded):
    bsz = k.shape[0]
    blocks_per_pair = HEAD_PAIRS // (KV_HEADS_B // 2)
    groups = SEQ // B_QGROUP
    q_spec = pl.BlockSpec((blocks_per_pair, B_QGROUP, LANES), lambda b, j, g: (j, b * groups + g, 0))
    scratch = [pltpu.VMEM((V_ROWS, SEQ), BF16)]
    if not bounded:
        scratch += [pltpu.VMEM((B_SLOTS, 2, SEQ, B_TQ), F32), pltpu.VMEM((B_SLOTS, 2, 1, B_TQ), F32)]
    return pl.pallas_call(
        functools.partial(_gqa_kernel, bounded=bounded),
        grid=(bsz, KV_HEADS_B // 2, groups),
        in_specs=[q_spec,
                  pl.BlockSpec((1, SEQ, LANES), lambda b, j, g: (b, 0, j)),
                  pl.BlockSpec((1, LANES, SEQ), lambda b, j, g: (b, j, 0)),
                  q_spec],
        out_specs=q_spec,
        out_shape=jax.ShapeDtypeStruct(q.shape, BF16),
        scratch_shapes=scratch,
        compiler_params=pltpu.CompilerParams(
            dimension_semantics=("parallel", "parallel", "arbitrary"),
            vmem_limit_bytes=VMEM_LIMIT),
        name="gqa_bounded" if bounded else "gqa",
    )(q, k, vt, gate)


def _pair_heads(w, axis):
    lead, trail = w.shape[:axis], w.shape[axis + 1:]
    group = HEADS // KV_HEADS_B
    w = w.reshape(*lead, KV_HEADS_B // 2, 2, group, HEAD_DIM, *trail)
    w = jnp.swapaxes(w, len(lead) + 1, len(lead) + 2)
    return w.reshape(*lead, D_MODEL, *trail)


def _rope_tables():
    t = np.arange(SEQ)
    pos = np.stack([t // GRID_W, t % GRID_W], axis=1).astype(np.float64)
    lane = np.arange(LANES) % HEAD_DIM
    section = lane // ROPE_SECTION
    n_freq = ROPE_SECTION // 2
    inv = np.power(ROPE_THETA, -np.arange(n_freq, dtype=np.float64) * 2.0 / ROPE_SECTION)
    ang = pos[:, section] * inv[lane % n_freq][None, :]
    sign = np.where(lane % ROPE_SECTION < n_freq, -1.0, 1.0)
    return jnp.asarray(np.cos(ang), F32), jnp.asarray(np.sin(ang) * sign[None, :], F32)


def kernel(x, p, norm_g, a_w_in, a_rpb, a_w_out, b_w_in, b_q_norm, b_k_norm, b_w_out, ple_norm_g, ple_w_gate,
           ple_w_proj, final_norm_g):
    bsz, seq, d = x.shape
    assert (seq, d) == (SEQ, D_MODEL)
    depth = p.shape[0]
    m = bsz * seq
    scale = math.log2(math.e) / math.sqrt(HEAD_DIM)
    xf = x.reshape(m, d)
    p_all = p.reshape(depth, m, PLE_DIM)
    cos, sin = _rope_tables()
    seg4 = np.kron(np.eye(4), np.ones((HEAD_DIM, HEAD_DIM)))
    seg = jnp.asarray(np.concatenate([seg4, seg4], axis=0), BF16)
    q_col_scale = np.ones((4 * D_MODEL,), np.float32)
    q_col_scale[:D_MODEL] = scale
    row2 = lambda v: v.reshape(1, -1)

    for i in range(depth):
        j = i // NUM_MIXERS
        if i % NUM_MIXERS == 0:
            w = (a_w_in[j] * q_col_scale).astype(BF16)
            q, k, vt, gate, n2 = _inproj_a(xf, row2(norm_g[i]), w, jnp.asarray(seg4, BF16))
            q, k, gate = (t.reshape(bsz, seq, D_MODEL) for t in (q, k, gate))
            rpb = a_rpb[j] * math.log2(math.e)
            slabs, slab_ids = _natten_bias(rpb)
            n2 = jnp.max(n2, axis=0)
            score_bound = 1.02 * jnp.sqrt(jnp.max(n2[0] * n2[1])) + jnp.max(jnp.abs(rpb))
            a = lax.cond(score_bound <= SCORE_BOUND,
                         lambda: _natten(q, k, vt, gate, slabs, slab_ids, bounded=True),
                         lambda: _natten(q, k, vt, gate, slabs, slab_ids, bounded=False)).reshape(m, D_MODEL)
            wo = a_w_out[j].astype(BF16)
        else:
            w = b_w_in[j]
            qk_w = D_MODEL + KV_W_B
            w = jnp.concatenate([_pair_heads(w[:, :D_MODEL], 1), w[:, D_MODEL:qk_w + KV_W_B],
                                 _pair_heads(w[:, qk_w + KV_W_B:], 1)], axis=1).astype(BF16)
            head_gain = jnp.concatenate([jnp.tile(b_q_norm[j] * scale, HEADS), jnp.tile(b_k_norm[j], KV_HEADS_B)])
            q, k, vt, gate = _inproj_b(xf, row2(norm_g[i]), w, row2(head_gain), cos, sin, seg)
            score_bound = HEAD_DIM * jnp.max(jnp.abs(b_q_norm[j] * scale)) * jnp.max(jnp.abs(b_k_norm[j]))
            k3 = k.reshape(bsz, seq, KV_W_B)
            a = lax.cond(score_bound <= SCORE_BOUND,
                         lambda: _gqa(q, k3, vt, gate, bounded=True),
                         lambda: _gqa(q, k3, vt, gate, bounded=False))
            wo = _pair_heads(b_w_out[j], 0).astype(BF16)
        xf = _outproj(a, xf, p_all, i, wo, row2(ple_norm_g[i]), ple_w_gate[i].astype(BF16),
                      ple_w_proj[i].astype(BF16), row2(final_norm_g), final_norm=(i == depth - 1))
    return xf.reshape(bsz, seq, d)
```

```python
import functools
import math

import jax
import jax.numpy as jnp
import numpy as np
from jax import lax
from jax.experimental import pallas as pl
from jax.experimental.pallas import tpu as pltpu

D_MODEL = 1024
SEQ = 4096
GRID_W = 64
GRID_ROWS = SEQ // GRID_W
HEAD_DIM = 64
HEADS = D_MODEL // HEAD_DIM
KV_HEADS_B = 4
KV_W_B = KV_HEADS_B * HEAD_DIM
WIN_ROWS = 8
WIN_COLS = 16
ROPE_THETA = 10000.0
ROPE_SECTION = HEAD_DIM // 2
PLE_DIM = 256
NORM_EPS = 1e-6
NUM_MIXERS = 2

LANES = 128
HEAD_PAIRS = HEADS // 2
NEG_BIAS = -1e30
ONES_ROWS = 16

ROW_TILE = 1024
A_QROWS = 4
A_KROWS = 12
A_KCHUNK = 4
A_TQ = A_QROWS * GRID_W
A_TK = A_KROWS * GRID_W
A_TKC = A_KCHUNK * GRID_W
A_NKC = A_KROWS // A_KCHUNK
A_BATCH = 2
A_UNROLL = 16
A_SLOTS, A_AHEAD = 4, 2
B_SLOTS, B_AHEAD = 2, 1
B_UNROLL = 4
SCORE_BOUND = 50.0
B_TQ = 256
B_QGROUP = 4096
B_TKC = 4096
VMEM_LIMIT = 56 * 1024 * 1024

BF16 = jnp.bfloat16
F32 = jnp.float32


def _rms(x, g):
    ms = jnp.mean(x * x, axis=-1, keepdims=True)
    return x * lax.rsqrt(ms + NORM_EPS) * g


def _dot(a, b):
    return jnp.dot(a, b, preferred_element_type=F32)


def _dot_nt(a, b):
    return lax.dot_general(a, b, (((1,), (1,)), ((), ())), preferred_element_type=F32)


def _head_masks():
    lo = (lax.broadcasted_iota(jnp.int32, (1, LANES), 1) < HEAD_DIM).astype(BF16)
    return lo, 1 - lo


def _silu(g):
    return g * jax.nn.sigmoid(g)


def _score_steps(st_ref, m_ref, slot, key_chunk, n_chunks, chunk_scores):
    maxes = ([], [])

    def chunk(c):
        keys = pl.ds(c * key_chunk, key_chunk)
        for hh in range(2):
            st = chunk_scores(hh, keys)
            st_ref[slot, hh, keys, :] = st
            maxes[hh].append(jnp.max(st, axis=0, keepdims=True))

    def done():
        for hh in range(2):
            m_ref[slot, hh] = functools.reduce(jnp.maximum, maxes[hh])

    return [functools.partial(chunk, c) for c in range(n_chunks)] + [done]


def _normalise(accs, shared):
    if shared:
        return jnp.concatenate([a[hh * HEAD_DIM:(hh + 1) * HEAD_DIM] / a[LANES:LANES + 1]
                                for hh, a in enumerate(accs)], axis=0)
    return jnp.concatenate([a[:HEAD_DIM] / a[HEAD_DIM:HEAD_DIM + 1] for a in accs], axis=0)


def _softmax_pv_steps(st_ref, m_ref, slot, key_chunk, n_chunks, vt_chunk, emit, shared):
    accs = [None, None]

    def chunk(c):
        keys = pl.ds(c * key_chunk, key_chunk)
        for hh in range(2):
            pt = jnp.exp2(st_ref[slot, hh, keys, :] - m_ref[slot, hh]).astype(BF16)
            pv = _dot(vt_chunk(hh, c), pt)
            accs[hh] = pv if accs[hh] is None else accs[hh] + pv

    def done():
        emit(_normalise(accs, shared))

    return [functools.partial(chunk, c) for c in range(n_chunks)] + [done]


def _pipeline(units, slots, ahead, scores, finish):
    assert units % slots == 0 and 0 < ahead < slots

    def run(*step_lists):
        for steps in zip(*step_lists):
            for step in reversed(steps):
                step()

    def trip(base, last):
        for k in range(slots):
            if not last or k + ahead < slots:
                run(scores(base + k + ahead, (k + ahead) % slots), finish(base + k, k))
            else:
                run(finish(base + k, k))

    for n in range(ahead):
        run(scores(n, n))

    def body(t, carry):
        trip(t * slots, False)
        return carry

    lax.fori_loop(0, units // slots - 1, body, 0)
    trip(units - slots, True)


def _inproj_a_kernel(x_ref, g_ref, w_ref, seg_ref, q_ref, k_ref, vt_ref, gate_ref, n2_ref):
    h = _rms(x_ref[...], g_ref[...]).astype(BF16)
    wide = 2 * LANES
    head_norms = []
    for c, o_ref in enumerate((q_ref, k_ref)):
        y = _dot(h, w_ref[:, c * D_MODEL:(c + 1) * D_MODEL])
        o_ref[...] = y.astype(BF16)
        ss = (y * y).astype(BF16)
        n2 = jnp.concatenate([_dot(ss[:, b * wide:(b + 1) * wide], seg_ref[...]) for b in range(D_MODEL // wide)],
                             axis=1)
        head_norms.append(jnp.max(n2, axis=0, keepdims=True))
    n2_ref[0] = jnp.concatenate(head_norms, axis=0)
    gate_ref[...] = _dot(h, w_ref[:, 3 * D_MODEL:]).astype(BF16)
    vt = lax.dot_general(w_ref[:, 2 * D_MODEL:3 * D_MODEL], h, (((0,), (1,)), ((), ())),
                         preferred_element_type=F32).astype(BF16)
    for c in range(vt_ref.shape[1]):
        vt_ref[0, c] = vt[:, c * A_TKC:(c + 1) * A_TKC]


def _inproj_a(x, g, w, seg):
    m = x.shape[0]
    out = jax.ShapeDtypeStruct((m, D_MODEL), BF16)
    tiles = SEQ // ROW_TILE
    fixed = lambda i: (0, 0)
    row_spec = pl.BlockSpec((ROW_TILE, D_MODEL), lambda i: (i, 0))
    return pl.pallas_call(
        _inproj_a_kernel,
        grid=(m // ROW_TILE,),
        in_specs=[row_spec,
                  pl.BlockSpec((1, D_MODEL), fixed),
                  pl.BlockSpec((D_MODEL, 4 * D_MODEL), fixed),
                  pl.BlockSpec((2 * LANES, 2 * LANES), fixed)],
        out_specs=[row_spec, row_spec,
                   pl.BlockSpec((1, ROW_TILE // A_TKC, D_MODEL, A_TKC), lambda i: (i // tiles, i % tiles, 0, 0)),
                   row_spec,
                   pl.BlockSpec((1, 2, D_MODEL), lambda i: (i, 0, 0))],
        out_shape=[out, out, jax.ShapeDtypeStruct((m // SEQ, SEQ // A_TKC, D_MODEL, A_TKC), BF16), out,
                   jax.ShapeDtypeStruct((m // ROW_TILE, 2, D_MODEL), F32)],
        compiler_params=pltpu.CompilerParams(dimension_semantics=("parallel",), vmem_limit_bytes=VMEM_LIMIT),
        name="inproj_a",
    )(x, g, w, seg)


def _inproj_b_kernel(x_ref, g_ref, w_ref, hg_ref, cos_ref, sin_ref, seg_ref, q_ref, k_ref, vt_ref, gate_ref):
    qk_w = D_MODEL + KV_W_B
    h = _rms(x_ref[...], g_ref[...]).astype(BF16)
    seg = seg_ref[...]
    cos = cos_ref[...]
    sin = sin_ref[...]
    first_half = (lax.broadcasted_iota(jnp.int32, (1, LANES), 1) % ROPE_SECTION) < (ROPE_SECTION // 2)
    yqk = _dot(h, w_ref[:, :qk_w])
    wide = 2 * LANES
    for pb in range(qk_w // wide):
        y = yqk[:, pb * wide:(pb + 1) * wide]
        ss = y * y
        hi = ss.astype(BF16)
        lo = (ss - hi.astype(F32)).astype(BF16)
        ms = _dot(jnp.concatenate([hi, lo], axis=1), seg) * (1.0 / HEAD_DIM)
        yn2 = y * lax.rsqrt(ms + NORM_EPS) * hg_ref[:, pb * wide:(pb + 1) * wide]
        for half in range(2):
            c = 2 * pb + half
            yn = yn2[:, half * LANES:(half + 1) * LANES]
            partner = jnp.where(first_half, pltpu.roll(yn, LANES - ROPE_SECTION // 2, 1),
                                pltpu.roll(yn, ROPE_SECTION // 2, 1))
            r = (yn * cos + partner * sin).astype(BF16)
            if c < D_MODEL // LANES:
                q_ref[c] = r
            else:
                k_ref[:, (c - D_MODEL // LANES) * LANES:(c - D_MODEL // LANES + 1) * LANES] = r
    vt_ref[0] = lax.dot_general(w_ref[:, qk_w:qk_w + KV_W_B], h, (((0,), (1,)), ((), ())),
                                preferred_element_type=F32).astype(BF16)
    gate = _dot(h, w_ref[:, qk_w + KV_W_B:])
    for c in range(D_MODEL // LANES):
        gate_ref[c] = gate[:, c * LANES:(c + 1) * LANES].astype(BF16)


def _inproj_b(x, g, w, head_gain, cos, sin, seg):
    m = x.shape[0]
    n_w = 2 * D_MODEL + 2 * KV_W_B
    tiles = SEQ // ROW_TILE
    row = lambda i: (i, 0)
    fixed = lambda i: (0, 0)
    pos = lambda i: (i % tiles, 0)
    wide = jax.ShapeDtypeStruct((D_MODEL // LANES, m, LANES), BF16)
    wide_spec = pl.BlockSpec((D_MODEL // LANES, ROW_TILE, LANES), lambda i: (0, i, 0))
    return pl.pallas_call(
        _inproj_b_kernel,
        grid=(m // ROW_TILE,),
        in_specs=[pl.BlockSpec((ROW_TILE, D_MODEL), row),
                  pl.BlockSpec((1, D_MODEL), fixed),
                  pl.BlockSpec((D_MODEL, n_w), fixed),
                  pl.BlockSpec((1, D_MODEL + KV_W_B), fixed),
                  pl.BlockSpec((ROW_TILE, LANES), pos),
                  pl.BlockSpec((ROW_TILE, LANES), pos),
                  pl.BlockSpec((4 * LANES, 2 * LANES), fixed)],
        out_specs=[wide_spec,
                   pl.BlockSpec((ROW_TILE, KV_W_B), row),
                   pl.BlockSpec((1, KV_W_B, ROW_TILE), lambda i: (i // tiles, 0, i % tiles)),
                   wide_spec],
        out_shape=[wide, jax.ShapeDtypeStruct((m, KV_W_B), BF16),
                   jax.ShapeDtypeStruct((m // SEQ, KV_W_B, SEQ), BF16), wide],
        compiler_params=pltpu.CompilerParams(dimension_semantics=("parallel",), vmem_limit_bytes=VMEM_LIMIT),
        name="inproj_b",
    )(x, g, w, head_gain, cos, sin, seg)


def _outproj_kernel(a_ref, x_ref, p_ref, wo_ref, pg_ref, wg_ref, wp_ref, fg_ref, o_ref, *, final_norm):
    if len(a_ref.shape) == 3:
        a = jnp.concatenate([a_ref[c] for c in range(a_ref.shape[0])], axis=1)
    else:
        a = a_ref[...]
    x1 = x_ref[...] + _dot(a, wo_ref[...])
    hn = _rms(x1, pg_ref[...]).astype(BF16)
    gate = jax.nn.sigmoid(_dot(hn, wg_ref[...]))
    x2 = x1 + gate * _dot(p_ref[...].astype(BF16), wp_ref[...])
    o_ref[...] = _rms(x2, fg_ref[...]) if final_norm else x2


def _outproj(a, x, p_all, layer, wo, pg, wg, wp, fg, final_norm):
    m = x.shape[0]
    row = lambda i: (i, 0)
    fixed = lambda i: (0, 0)
    return pl.pallas_call(
        functools.partial(_outproj_kernel, final_norm=final_norm),
        grid=(m // ROW_TILE,),
        in_specs=[pl.BlockSpec((ROW_TILE, D_MODEL), row) if a.ndim == 2 else
                  pl.BlockSpec((a.shape[0], ROW_TILE, LANES), lambda i: (0, i, 0)),
                  pl.BlockSpec((ROW_TILE, D_MODEL), row),
                  pl.BlockSpec((None, ROW_TILE, PLE_DIM), lambda i: (layer, i, 0)),
                  pl.BlockSpec((D_MODEL, D_MODEL), fixed),
                  pl.BlockSpec((1, D_MODEL), fixed),
                  pl.BlockSpec((D_MODEL, D_MODEL), fixed),
                  pl.BlockSpec((PLE_DIM, D_MODEL), fixed),
                  pl.BlockSpec((1, D_MODEL), fixed)],
        out_specs=pl.BlockSpec((ROW_TILE, D_MODEL), row),
        out_shape=jax.ShapeDtypeStruct((m, D_MODEL), F32),
        compiler_params=pltpu.CompilerParams(dimension_semantics=("parallel",), vmem_limit_bytes=VMEM_LIMIT),
        name="outproj_final" if final_norm else "outproj",
    )(a, x, p_all, wo, pg, wg, wp, fg)


def _natten_kernel(slab_ids_ref, q_ref, k_ref, vt_ref, gate_ref, bias_ref, o_ref, *scratch, bounded):
    n_blocks = GRID_ROWS // A_QROWS
    masks = _head_masks()
    ones = jnp.ones((ONES_ROWS, A_TK), BF16)

    def where(n):
        n = jnp.asarray(n, jnp.int32)
        bb, i = n // n_blocks, n % n_blocks
        first_chunk = jnp.clip(i - 1, 0, n_blocks - A_NKC)
        variant = jnp.where(i == 0, 0, jnp.where(i == n_blocks - 1, 2, 1))
        return bb, pl.ds(pl.multiple_of(i * A_TQ, A_TQ), A_TQ), first_chunk, variant

    def unit_scores(n):
        bb, q_rows, first_chunk, variant = where(n)
        q2 = q_ref[bb, q_rows, :]
        kwin = k_ref[bb, pl.ds(pl.multiple_of(first_chunk * A_TKC, A_TKC), A_TK), :]
        out = []
        for hh in range(2):
            bias = jnp.concatenate(
                [jnp.concatenate([bias_ref[hh, slab_ids_ref[(variant * A_KROWS + r) * 2 + half]]
                                  for half in range(2)], axis=1) for r in range(A_KROWS)], axis=0)
            out.append(_dot_nt(kwin, q2 * masks[hh]) + bias)
        return out

    def vt_ext(n):
        bb, _, first_chunk, _ = where(n)
        vt = jnp.concatenate([vt_ref[bb, first_chunk + kc] for kc in range(A_NKC)], axis=1)
        return jnp.concatenate([vt, ones], axis=0)

    def emit(n, ot):
        bb, q_rows, _, _ = where(n)
        o_ref[bb, q_rows, :] = (ot.T * _silu(gate_ref[bb, q_rows, :].astype(F32))).astype(BF16)

    units = q_ref.shape[0] * n_blocks
    if bounded:
        def body(t, carry):
            st = {0: unit_scores(t * A_UNROLL)}
            for u in range(A_UNROLL):
                n = t * A_UNROLL + u
                if u + 1 < A_UNROLL:
                    st[u + 1] = unit_scores(n + 1)
                vt = vt_ext(n)
                emit(n, _normalise([_dot(vt, jnp.exp2(sc).astype(BF16)) for sc in st.pop(u)], shared=True))
            return carry

        lax.fori_loop(0, units // A_UNROLL, body, 0)
        return

    st_ref, m_ref = scratch

    def scores(n, slot):
        sc = unit_scores(n)
        return _score_steps(st_ref, m_ref, slot, A_TK, 1, lambda hh, keys: sc[hh])

    def finish(n, slot):
        return _softmax_pv_steps(st_ref, m_ref, slot, A_TK, 1, lambda hh, c: vt_ext(n), functools.partial(emit, n),
                                 shared=True)

    _pipeline(units, st_ref.shape[0], A_AHEAD, scores, finish)


def _natten(q, k, vt, gate, slabs, slab_ids, bounded):
    bsz = q.shape[0]
    tok_spec = pl.BlockSpec((A_BATCH, SEQ, LANES), lambda p, g: (g, 0, p))
    scratch = [] if bounded else [pltpu.VMEM((A_SLOTS, 2, A_TK, A_TQ), F32),
                                  pltpu.VMEM((A_SLOTS, 2, 1, A_TQ), F32)]
    return pl.pallas_call(
        functools.partial(_natten_kernel, bounded=bounded),
        grid=(HEAD_PAIRS, bsz // A_BATCH),
        in_specs=[pl.BlockSpec(memory_space=pltpu.SMEM),
                  tok_spec, tok_spec,
                  pl.BlockSpec((A_BATCH, SEQ // A_TKC, LANES, A_TKC), lambda p, g: (g, 0, p, 0)),
                  tok_spec,
                  pl.BlockSpec((2,) + slabs.shape[1:], lambda p, g: (p, 0, 0, 0))],
        out_specs=tok_spec,
        out_shape=jax.ShapeDtypeStruct(q.shape, BF16),
        scratch_shapes=scratch,
        compiler_params=pltpu.CompilerParams(dimension_semantics=("parallel", "parallel"),
                                             vmem_limit_bytes=VMEM_LIMIT),
        name="natten_bounded" if bounded else "natten",
    )(slab_ids, q, k, vt, gate, slabs)


def _natten_bias(rpb):
    qc = np.arange(GRID_W)
    kc = np.arange(GRID_W)
    cs = np.clip(qc - WIN_COLS // 2, 0, GRID_W - WIN_COLS)
    col_ok = (kc[:, None] >= cs[None, :]) & (kc[:, None] < cs[None, :] + WIN_COLS)
    dc = np.clip(kc[:, None] - qc[None, :] + WIN_COLS - 1, 0, 2 * WIN_COLS - 2)
    table = jnp.where(col_ok[None, None], jnp.take(rpb, jnp.asarray(dc), axis=2), NEG_BIAS)
    n_rel = 2 * WIN_ROWS - 1
    table = jnp.concatenate([table, jnp.full((HEADS, 1, GRID_W, GRID_W), NEG_BIAS, F32)], axis=1)
    n_blocks = GRID_ROWS // A_QROWS
    pairs, ids = [], []
    for i in (0, 1, n_blocks - 1):
        ws = A_KCHUNK * min(max(i - 1, 0), n_blocks - A_NKC)
        for b in range(A_KROWS):
            for half in range(A_QROWS // 2):
                rel = []
                for a in (2 * half, 2 * half + 1):
                    qr = A_QROWS * i + a
                    rs = min(max(qr - WIN_ROWS // 2, 0), GRID_ROWS - WIN_ROWS)
                    rel.append(ws + b - qr + WIN_ROWS - 1 if rs <= ws + b < rs + WIN_ROWS else n_rel)
                if tuple(rel) not in pairs:
                    pairs.append(tuple(rel))
                ids.append(pairs.index(tuple(rel)))
    slabs = jnp.stack([jnp.concatenate([table[:, left], table[:, right]], axis=-1) for left, right in pairs], axis=1)
    return slabs, jnp.asarray(ids, jnp.int32)


def _gqa_kernel(q_ref, k_ref, vt_ref, gate_ref, o_ref, vt_ext_ref, *scratch, bounded):
    blocks, qgroup, _ = q_ref.shape
    units = blocks * (qgroup // B_TQ)
    n_chunks = SEQ // B_TKC

    @pl.when(pl.program_id(2) == 0)
    def _():
        for hh in range(2):
            vt_ext_ref[hh, :HEAD_DIM] = vt_ref[0, hh * HEAD_DIM:(hh + 1) * HEAD_DIM, :]
            vt_ext_ref[hh, HEAD_DIM:] = jnp.ones((LANES - HEAD_DIM, SEQ), BF16)

    masks = _head_masks()

    def rows(n):
        start = (n // blocks) * B_TQ
        return pl.ds(start if isinstance(n, int) else pl.multiple_of(start, B_TQ), B_TQ)

    def emit(n, ot):
        g = gate_ref[n % blocks, rows(n), :].astype(F32)
        o_ref[n % blocks, rows(n), :] = (ot.T * _silu(g)).astype(BF16)

    def masked_q(n):
        q2 = q_ref[n % blocks, rows(n), :]
        return [q2 * masks[hh] for hh in range(2)]

    def vt_chunk(hh, c):
        return vt_ext_ref[hh, :, c * B_TKC:(c + 1) * B_TKC]

    if bounded:
        def body(t, carry):
            items = [(t * B_UNROLL + u, c) for u in range(B_UNROLL) for c in range(n_chunks)]
            qm, st, accs = {}, {}, {}

            def qk(idx):
                n, c = items[idx]
                if c == 0:
                    qm[idx // n_chunks] = masked_q(n)
                for hh in range(2):
                    st[idx, hh] = _dot_nt(k_ref[0, c * B_TKC:(c + 1) * B_TKC, :], qm[idx // n_chunks][hh])

            qk(0)
            for idx, (n, c) in enumerate(items):
                if idx + 1 < len(items):
                    qk(idx + 1)
                for hh in range(2):
                    pv = _dot(vt_chunk(hh, c), jnp.exp2(st.pop((idx, hh))).astype(BF16))
                    accs[hh] = pv if c == 0 else accs[hh] + pv
                if c == n_chunks - 1:
                    emit(n, _normalise([accs[0], accs[1]], shared=False))
            return carry

        lax.fori_loop(0, units // B_UNROLL, body, 0)
        return

    st_ref, m_ref = scratch

    def scores(n, slot):
        qm = masked_q(n)
        return _score_steps(st_ref, m_ref, slot, B_TKC, n_chunks,
                            lambda hh, keys: _dot_nt(k_ref[0, keys, :], qm[hh]))

    def finish(n, slot):
        return _softmax_pv_steps(st_ref, m_ref, slot, B_TKC, n_chunks, vt_chunk, functools.partial(emit, n),
                                 shared=False)

    _pipeline(units, st_ref.shape[0], B_AHEAD, scores, finish)


def _gqa(q, k, vt, gate, bounded):
    bsz = k.shape[0]
    blocks_per_pair = HEAD_PAIRS // (KV_HEADS_B // 2)
    groups = SEQ // B_QGROUP
    q_spec = pl.BlockSpec((blocks_per_pair, B_QGROUP, LANES), lambda b, j, g: (j, b * groups + g, 0))
    scratch = [pltpu.VMEM((2, LANES, SEQ), BF16)]
    if not bounded:
        scratch += [pltpu.VMEM((B_SLOTS, 2, SEQ, B_TQ), F32), pltpu.VMEM((B_SLOTS, 2, 1, B_TQ), F32)]
    return pl.pallas_call(
        functools.partial(_gqa_kernel, bounded=bounded),
        grid=(bsz, KV_HEADS_B // 2, groups),
        in_specs=[q_spec,
                  pl.BlockSpec((1, SEQ, LANES), lambda b, j, g: (b, 0, j)),
                  pl.BlockSpec((1, LANES, SEQ), lambda b, j, g: (b, j, 0)),
                  q_spec],
        out_specs=q_spec,
        out_shape=jax.ShapeDtypeStruct(q.shape, BF16),
        scratch_shapes=scratch,
        compiler_params=pltpu.CompilerParams(
            dimension_semantics=("parallel", "parallel", "arbitrary"),
            vmem_limit_bytes=VMEM_LIMIT),
        name="gqa_bounded" if bounded else "gqa",
    )(q, k, vt, gate)


def _pair_heads(w, axis):
    lead, trail = w.shape[:axis], w.shape[axis + 1:]
    group = HEADS // KV_HEADS_B
    w = w.reshape(*lead, KV_HEADS_B // 2, 2, group, HEAD_DIM, *trail)
    w = jnp.swapaxes(w, len(lead) + 1, len(lead) + 2)
    return w.reshape(*lead, D_MODEL, *trail)


def _rope_tables():
    t = np.arange(SEQ)
    pos = np.stack([t // GRID_W, t % GRID_W], axis=1).astype(np.float64)
    lane = np.arange(LANES) % HEAD_DIM
    section = lane // ROPE_SECTION
    n_freq = ROPE_SECTION // 2
    inv = np.power(ROPE_THETA, -np.arange(n_freq, dtype=np.float64) * 2.0 / ROPE_SECTION)
    ang = pos[:, section] * inv[lane % n_freq][None, :]
    sign = np.where(lane % ROPE_SECTION < n_freq, -1.0, 1.0)
    return jnp.asarray(np.cos(ang), F32), jnp.asarray(np.sin(ang) * sign[None, :], F32)


def kernel(x, p, norm_g, a_w_in, a_rpb, a_w_out, b_w_in, b_q_norm, b_k_norm, b_w_out, ple_norm_g, ple_w_gate,
           ple_w_proj, final_norm_g):
    bsz, seq, d = x.shape
    assert (seq, d) == (SEQ, D_MODEL)
    depth = p.shape[0]
    m = bsz * seq
    scale = math.log2(math.e) / math.sqrt(HEAD_DIM)
    xf = x.reshape(m, d)
    p_all = p.reshape(depth, m, PLE_DIM)
    cos, sin = _rope_tables()
    seg4 = np.kron(np.eye(4), np.ones((HEAD_DIM, HEAD_DIM)))
    seg = jnp.asarray(np.concatenate([seg4, seg4], axis=0), BF16)
    q_col_scale = np.ones((4 * D_MODEL,), np.float32)
    q_col_scale[:D_MODEL] = scale
    row2 = lambda v: v.reshape(1, -1)

    for i in range(depth):
        j = i // NUM_MIXERS
        if i % NUM_MIXERS == 0:
            w = (a_w_in[j] * q_col_scale).astype(BF16)
            q, k, vt, gate, n2 = _inproj_a(xf, row2(norm_g[i]), w, jnp.asarray(seg4, BF16))
            q, k, gate = (t.reshape(bsz, seq, D_MODEL) for t in (q, k, gate))
            rpb = a_rpb[j] * math.log2(math.e)
            slabs, slab_ids = _natten_bias(rpb)
            n2 = jnp.max(n2, axis=0)
            score_bound = 1.02 * jnp.sqrt(jnp.max(n2[0] * n2[1])) + jnp.max(jnp.abs(rpb))
            a = lax.cond(score_bound <= SCORE_BOUND,
                         lambda: _natten(q, k, vt, gate, slabs, slab_ids, bounded=True),
                         lambda: _natten(q, k, vt, gate, slabs, slab_ids, bounded=False)).reshape(m, D_MODEL)
            wo = a_w_out[j].astype(BF16)
        else:
            w = b_w_in[j]
            qk_w = D_MODEL + KV_W_B
            w = jnp.concatenate([_pair_heads(w[:, :D_MODEL], 1), w[:, D_MODEL:qk_w + KV_W_B],
                                 _pair_heads(w[:, qk_w + KV_W_B:], 1)], axis=1).astype(BF16)
            head_gain = jnp.concatenate([jnp.tile(b_q_norm[j] * scale, HEADS), jnp.tile(b_k_norm[j], KV_HEADS_B)])
            q, k, vt, gate = _inproj_b(xf, row2(norm_g[i]), w, row2(head_gain), cos, sin, seg)
            score_bound = HEAD_DIM * jnp.max(jnp.abs(b_q_norm[j] * scale)) * jnp.max(jnp.abs(b_k_norm[j]))
            k3 = k.reshape(bsz, seq, KV_W_B)
            a = lax.cond(score_bound <= SCORE_BOUND,
                         lambda: _gqa(q, k3, vt, gate, bounded=True),
                         lambda: _gqa(q, k3, vt, gate, bounded=False))
            wo = _pair_heads(b_w_out[j], 0).astype(BF16)
        xf = _outproj(a, xf, p_all, i, wo, row2(ple_norm_g[i]), ple_w_gate[i].astype(BF16),
                      ple_w_proj[i].astype(BF16), row2(final_norm_g), final_norm=(i == depth - 1))
    return xf.reshape(bsz, seq, d)
```

```python
import functools
import math

import jax
import jax.numpy as jnp
import numpy as np
from jax import lax
from jax.experimental import pallas as pl
from jax.experimental.pallas import tpu as pltpu

D_MODEL = 1024
SEQ = 4096
GRID_W = 64
GRID_ROWS = SEQ // GRID_W
HEAD_DIM = 64
HEADS = D_MODEL // HEAD_DIM
KV_HEADS_B = 4
KV_W_B = KV_HEADS_B * HEAD_DIM
WIN_ROWS = 8
WIN_COLS = 16
ROPE_THETA = 10000.0
ROPE_SECTION = HEAD_DIM // 2
PLE_DIM = 256
NORM_EPS = 1e-6
NUM_MIXERS = 2

LANES = 128
HEAD_PAIRS = HEADS // 2
NEG_BIAS = -1e30

ROW_TILE = 1024
A_QROWS = 4
A_KROWS = 12
A_KCHUNK = 4
A_TQ = A_QROWS * GRID_W
A_TK = A_KROWS * GRID_W
A_TKC = A_KCHUNK * GRID_W
A_NKC = A_KROWS // A_KCHUNK
A_BATCH = 2
A_UNROLL = 16
A_SLOTS, A_AHEAD = 4, 2
B_SLOTS, B_AHEAD = 2, 1
B_UNROLL = 8
SCORE_BOUND = 50.0
B_TQ = 256
B_QGROUP = 4096
B_TKC = 4096
VMEM_LIMIT = 56 * 1024 * 1024

BF16 = jnp.bfloat16
F32 = jnp.float32


def _rms(x, g):
    ms = jnp.mean(x * x, axis=-1, keepdims=True)
    return x * lax.rsqrt(ms + NORM_EPS) * g


def _dot(a, b):
    return jnp.dot(a, b, preferred_element_type=F32)


def _dot_nt(a, b):
    return lax.dot_general(a, b, (((1,), (1,)), ((), ())), preferred_element_type=F32)


def _head_masks():
    lo = (lax.broadcasted_iota(jnp.int32, (1, LANES), 1) < HEAD_DIM).astype(BF16)
    return lo, 1 - lo


def _silu(g):
    return g * jax.nn.sigmoid(g)


def _score_steps(st_ref, m_ref, slot, key_chunk, n_chunks, chunk_scores):
    maxes = ([], [])

    def chunk(c):
        keys = pl.ds(c * key_chunk, key_chunk)
        for hh in range(2):
            st = chunk_scores(hh, keys)
            st_ref[slot, hh, keys, :] = st
            maxes[hh].append(jnp.max(st, axis=0, keepdims=True))

    def done():
        for hh in range(2):
            m_ref[slot, hh] = functools.reduce(jnp.maximum, maxes[hh])

    return [functools.partial(chunk, c) for c in range(n_chunks)] + [done]


def _normalise(accs):
    return jnp.concatenate([a[:HEAD_DIM] / a[HEAD_DIM:HEAD_DIM + 1] for a in accs], axis=0)


def _softmax_pv_steps(st_ref, m_ref, slot, key_chunk, n_chunks, vt_chunk, emit):
    accs = [None, None]

    def chunk(c):
        keys = pl.ds(c * key_chunk, key_chunk)
        for hh in range(2):
            pt = jnp.exp2(st_ref[slot, hh, keys, :] - m_ref[slot, hh]).astype(BF16)
            pv = _dot(vt_chunk(hh, c), pt)
            accs[hh] = pv if accs[hh] is None else accs[hh] + pv

    def done():
        emit(_normalise(accs))

    return [functools.partial(chunk, c) for c in range(n_chunks)] + [done]


def _pipeline(units, slots, ahead, scores, finish):
    assert units % slots == 0 and 0 < ahead < slots

    def run(*step_lists):
        for steps in zip(*step_lists):
            for step in reversed(steps):
                step()

    def trip(base, last):
        for k in range(slots):
            if not last or k + ahead < slots:
                run(scores(base + k + ahead, (k + ahead) % slots), finish(base + k, k))
            else:
                run(finish(base + k, k))

    for n in range(ahead):
        run(scores(n, n))

    def body(t, carry):
        trip(t * slots, False)
        return carry

    lax.fori_loop(0, units // slots - 1, body, 0)
    trip(units - slots, True)


def _inproj_a_kernel(x_ref, g_ref, w_ref, seg_ref, q_ref, k_ref, vt_ref, gate_ref, n2_ref):
    h = _rms(x_ref[...], g_ref[...]).astype(BF16)
    wide = 2 * LANES
    head_norms = []
    for c, o_ref in enumerate((q_ref, k_ref)):
        y = _dot(h, w_ref[:, c * D_MODEL:(c + 1) * D_MODEL])
        o_ref[...] = y.astype(BF16)
        ss = (y * y).astype(BF16)
        n2 = jnp.concatenate([_dot(ss[:, b * wide:(b + 1) * wide], seg_ref[...]) for b in range(D_MODEL // wide)],
                             axis=1)
        head_norms.append(jnp.max(n2, axis=0, keepdims=True))
    n2_ref[0] = jnp.concatenate(head_norms, axis=0)
    gate_ref[...] = _dot(h, w_ref[:, 3 * D_MODEL:]).astype(BF16)
    vt = lax.dot_general(w_ref[:, 2 * D_MODEL:3 * D_MODEL], h, (((0,), (1,)), ((), ())),
                         preferred_element_type=F32).astype(BF16)
    for c in range(vt_ref.shape[1]):
        vt_ref[0, c] = vt[:, c * A_TKC:(c + 1) * A_TKC]


def _inproj_a(x, g, w, seg):
    m = x.shape[0]
    out = jax.ShapeDtypeStruct((m, D_MODEL), BF16)
    tiles = SEQ // ROW_TILE
    fixed = lambda i: (0, 0)
    row_spec = pl.BlockSpec((ROW_TILE, D_MODEL), lambda i: (i, 0))
    return pl.pallas_call(
        _inproj_a_kernel,
        grid=(m // ROW_TILE,),
        in_specs=[row_spec,
                  pl.BlockSpec((1, D_MODEL), fixed),
                  pl.BlockSpec((D_MODEL, 4 * D_MODEL), fixed),
                  pl.BlockSpec((2 * LANES, 2 * LANES), fixed)],
        out_specs=[row_spec, row_spec,
                   pl.BlockSpec((1, ROW_TILE // A_TKC, D_MODEL, A_TKC), lambda i: (i // tiles, i % tiles, 0, 0)),
                   row_spec,
                   pl.BlockSpec((1, 2, D_MODEL), lambda i: (i, 0, 0))],
        out_shape=[out, out, jax.ShapeDtypeStruct((m // SEQ, SEQ // A_TKC, D_MODEL, A_TKC), BF16), out,
                   jax.ShapeDtypeStruct((m // ROW_TILE, 2, D_MODEL), F32)],
        compiler_params=pltpu.CompilerParams(dimension_semantics=("parallel",), vmem_limit_bytes=VMEM_LIMIT),
        name="inproj_a",
    )(x, g, w, seg)


def _inproj_b_kernel(x_ref, g_ref, w_ref, hg_ref, cos_ref, sin_ref, seg_ref, q_ref, k_ref, vt_ref, gate_ref):
    qk_w = D_MODEL + KV_W_B
    h = _rms(x_ref[...], g_ref[...]).astype(BF16)
    seg = seg_ref[...]
    cos = cos_ref[...]
    sin = sin_ref[...]
    first_half = (lax.broadcasted_iota(jnp.int32, (1, LANES), 1) % ROPE_SECTION) < (ROPE_SECTION // 2)
    yqk = _dot(h, w_ref[:, :qk_w])
    wide = 2 * LANES
    for pb in range(qk_w // wide):
        y = yqk[:, pb * wide:(pb + 1) * wide]
        ss = y * y
        hi = ss.astype(BF16)
        lo = (ss - hi.astype(F32)).astype(BF16)
        ms = _dot(jnp.concatenate([hi, lo], axis=1), seg) * (1.0 / HEAD_DIM)
        yn2 = y * lax.rsqrt(ms + NORM_EPS) * hg_ref[:, pb * wide:(pb + 1) * wide]
        for half in range(2):
            c = 2 * pb + half
            yn = yn2[:, half * LANES:(half + 1) * LANES]
            partner = jnp.where(first_half, pltpu.roll(yn, LANES - ROPE_SECTION // 2, 1),
                                pltpu.roll(yn, ROPE_SECTION // 2, 1))
            r = (yn * cos + partner * sin).astype(BF16)
            if c < D_MODEL // LANES:
                q_ref[c] = r
            else:
                k_ref[:, (c - D_MODEL // LANES) * LANES:(c - D_MODEL // LANES + 1) * LANES] = r
    vt_ref[0] = lax.dot_general(w_ref[:, qk_w:qk_w + KV_W_B], h, (((0,), (1,)), ((), ())),
                                preferred_element_type=F32).astype(BF16)
    gate = _dot(h, w_ref[:, qk_w + KV_W_B:])
    for c in range(D_MODEL // LANES):
        gate_ref[c] = gate[:, c * LANES:(c + 1) * LANES].astype(BF16)


def _inproj_b(x, g, w, head_gain, cos, sin, seg):
    m = x.shape[0]
    n_w = 2 * D_MODEL + 2 * KV_W_B
    tiles = SEQ // ROW_TILE
    row = lambda i: (i, 0)
    fixed = lambda i: (0, 0)
    pos = lambda i: (i % tiles, 0)
    wide = jax.ShapeDtypeStruct((D_MODEL // LANES, m, LANES), BF16)
    wide_spec = pl.BlockSpec((D_MODEL // LANES, ROW_TILE, LANES), lambda i: (0, i, 0))
    return pl.pallas_call(
        _inproj_b_kernel,
        grid=(m // ROW_TILE,),
        in_specs=[pl.BlockSpec((ROW_TILE, D_MODEL), row),
                  pl.BlockSpec((1, D_MODEL), fixed),
                  pl.BlockSpec((D_MODEL, n_w), fixed),
                  pl.BlockSpec((1, D_MODEL + KV_W_B), fixed),
                  pl.BlockSpec((ROW_TILE, LANES), pos),
                  pl.BlockSpec((ROW_TILE, LANES), pos),
                  pl.BlockSpec((4 * LANES, 2 * LANES), fixed)],
        out_specs=[wide_spec,
                   pl.BlockSpec((ROW_TILE, KV_W_B), row),
                   pl.BlockSpec((1, KV_W_B, ROW_TILE), lambda i: (i // tiles, 0, i % tiles)),
                   wide_spec],
        out_shape=[wide, jax.ShapeDtypeStruct((m, KV_W_B), BF16),
                   jax.ShapeDtypeStruct((m // SEQ, KV_W_B, SEQ), BF16), wide],
        compiler_params=pltpu.CompilerParams(dimension_semantics=("parallel",), vmem_limit_bytes=VMEM_LIMIT),
        name="inproj_b",
    )(x, g, w, head_gain, cos, sin, seg)


def _outproj_kernel(a_ref, x_ref, p_ref, wo_ref, pg_ref, wg_ref, wp_ref, fg_ref, o_ref, *, final_norm):
    if len(a_ref.shape) == 3:
        a = jnp.concatenate([a_ref[c] for c in range(a_ref.shape[0])], axis=1)
    else:
        a = a_ref[...]
    x1 = x_ref[...] + _dot(a, wo_ref[...])
    hn = _rms(x1, pg_ref[...]).astype(BF16)
    gate = jax.nn.sigmoid(_dot(hn, wg_ref[...]))
    x2 = x1 + gate * _dot(p_ref[...].astype(BF16), wp_ref[...])
    o_ref[...] = _rms(x2, fg_ref[...]) if final_norm else x2


def _outproj(a, x, p_all, layer, wo, pg, wg, wp, fg, final_norm):
    m = x.shape[0]
    row = lambda i: (i, 0)
    fixed = lambda i: (0, 0)
    return pl.pallas_call(
        functools.partial(_outproj_kernel, final_norm=final_norm),
        grid=(m // ROW_TILE,),
        in_specs=[pl.BlockSpec((ROW_TILE, D_MODEL), row) if a.ndim == 2 else
                  pl.BlockSpec((a.shape[0], ROW_TILE, LANES), lambda i: (0, i, 0)),
                  pl.BlockSpec((ROW_TILE, D_MODEL), row),
                  pl.BlockSpec((None, ROW_TILE, PLE_DIM), lambda i: (layer, i, 0)),
                  pl.BlockSpec((D_MODEL, D_MODEL), fixed),
                  pl.BlockSpec((1, D_MODEL), fixed),
                  pl.BlockSpec((D_MODEL, D_MODEL), fixed),
                  pl.BlockSpec((PLE_DIM, D_MODEL), fixed),
                  pl.BlockSpec((1, D_MODEL), fixed)],
        out_specs=pl.BlockSpec((ROW_TILE, D_MODEL), row),
        out_shape=jax.ShapeDtypeStruct((m, D_MODEL), F32),
        compiler_params=pltpu.CompilerParams(dimension_semantics=("parallel",), vmem_limit_bytes=VMEM_LIMIT),
        name="outproj_final" if final_norm else "outproj",
    )(a, x, p_all, wo, pg, wg, wp, fg)


def _natten_kernel(slab_ids_ref, q_ref, k_ref, vt_ref, gate_ref, bias_ref, o_ref, *scratch, bounded):
    n_blocks = GRID_ROWS // A_QROWS
    masks = _head_masks()
    ones = jnp.ones((LANES - HEAD_DIM, A_TK), BF16)

    def where(n):
        n = jnp.asarray(n, jnp.int32)
        bb, i = n // n_blocks, n % n_blocks
        first_chunk = jnp.clip(i - 1, 0, n_blocks - A_NKC)
        variant = jnp.where(i == 0, 0, jnp.where(i == n_blocks - 1, 2, 1))
        return bb, pl.ds(pl.multiple_of(i * A_TQ, A_TQ), A_TQ), first_chunk, variant

    def unit_scores(n):
        bb, q_rows, first_chunk, variant = where(n)
        q2 = q_ref[bb, q_rows, :]
        kwin = k_ref[bb, pl.ds(pl.multiple_of(first_chunk * A_TKC, A_TKC), A_TK), :]
        out = []
        for hh in range(2):
            bias = jnp.concatenate(
                [jnp.concatenate([bias_ref[hh, slab_ids_ref[(variant * A_KROWS + r) * 2 + half]]
                                  for half in range(2)], axis=1) for r in range(A_KROWS)], axis=0)
            out.append(_dot_nt(kwin, q2 * masks[hh]) + bias)
        return out

    def vt_ext(n):
        bb, _, first_chunk, _ = where(n)
        vt = jnp.concatenate([vt_ref[bb, first_chunk + kc] for kc in range(A_NKC)], axis=1)
        return [jnp.concatenate([vt[hh * HEAD_DIM:(hh + 1) * HEAD_DIM], ones], axis=0) for hh in range(2)]

    def emit(n, ot):
        bb, q_rows, _, _ = where(n)
        o_ref[bb, q_rows, :] = (ot.T * _silu(gate_ref[bb, q_rows, :].astype(F32))).astype(BF16)

    units = q_ref.shape[0] * n_blocks
    if bounded:
        def body(t, carry):
            st = {0: unit_scores(t * A_UNROLL)}
            for u in range(A_UNROLL):
                n = t * A_UNROLL + u
                if u + 1 < A_UNROLL:
                    st[u + 1] = unit_scores(n + 1)
                vt = vt_ext(n)
                emit(n, _normalise([_dot(v, jnp.exp2(sc).astype(BF16)) for v, sc in zip(vt, st.pop(u))]))
            return carry

        lax.fori_loop(0, units // A_UNROLL, body, 0)
        return

    st_ref, m_ref = scratch

    def scores(n, slot):
        sc = unit_scores(n)
        return _score_steps(st_ref, m_ref, slot, A_TK, 1, lambda hh, keys: sc[hh])

    def finish(n, slot):
        vt = vt_ext(n)
        return _softmax_pv_steps(st_ref, m_ref, slot, A_TK, 1, lambda hh, c: vt[hh], functools.partial(emit, n))

    _pipeline(units, st_ref.shape[0], A_AHEAD, scores, finish)


def _natten(q, k, vt, gate, slabs, slab_ids, bounded):
    bsz = q.shape[0]
    tok_spec = pl.BlockSpec((A_BATCH, SEQ, LANES), lambda p, g: (g, 0, p))
    scratch = [] if bounded else [pltpu.VMEM((A_SLOTS, 2, A_TK, A_TQ), F32),
                                  pltpu.VMEM((A_SLOTS, 2, 1, A_TQ), F32)]
    return pl.pallas_call(
        functools.partial(_natten_kernel, bounded=bounded),
        grid=(HEAD_PAIRS, bsz // A_BATCH),
        in_specs=[pl.BlockSpec(memory_space=pltpu.SMEM),
                  tok_spec, tok_spec,
                  pl.BlockSpec((A_BATCH, SEQ // A_TKC, LANES, A_TKC), lambda p, g: (g, 0, p, 0)),
                  tok_spec,
                  pl.BlockSpec((2,) + slabs.shape[1:], lambda p, g: (p, 0, 0, 0))],
        out_specs=tok_spec,
        out_shape=jax.ShapeDtypeStruct(q.shape, BF16),
        scratch_shapes=scratch,
        compiler_params=pltpu.CompilerParams(dimension_semantics=("parallel", "parallel"),
                                             vmem_limit_bytes=VMEM_LIMIT),
        name="natten_bounded" if bounded else "natten",
    )(slab_ids, q, k, vt, gate, slabs)


def _natten_bias(rpb):
    qc = np.arange(GRID_W)
    kc = np.arange(GRID_W)
    cs = np.clip(qc - WIN_COLS // 2, 0, GRID_W - WIN_COLS)
    col_ok = (kc[:, None] >= cs[None, :]) & (kc[:, None] < cs[None, :] + WIN_COLS)
    n_rel, n_col = 2 * WIN_ROWS - 1, 2 * WIN_COLS - 1
    dc = np.clip(kc[:, None] - qc[None, :] + WIN_COLS - 1, 0, n_col - 1)
    select = (np.arange(n_col)[:, None] == dc.reshape(1, -1)).astype(np.float32)
    table = jnp.dot(rpb.reshape(HEADS * n_rel, n_col), select, precision=lax.Precision.HIGHEST)
    table = jnp.where(col_ok[None, None], table.reshape(HEADS, n_rel, GRID_W, GRID_W), NEG_BIAS)
    table = jnp.concatenate([table, jnp.full((HEADS, 1, GRID_W, GRID_W), NEG_BIAS, F32)], axis=1)
    n_blocks = GRID_ROWS // A_QROWS
    pairs, ids = [], []
    for i in (0, 1, n_blocks - 1):
        ws = A_KCHUNK * min(max(i - 1, 0), n_blocks - A_NKC)
        for b in range(A_KROWS):
            for half in range(A_QROWS // 2):
                rel = []
                for a in (2 * half, 2 * half + 1):
                    qr = A_QROWS * i + a
                    rs = min(max(qr - WIN_ROWS // 2, 0), GRID_ROWS - WIN_ROWS)
                    rel.append(ws + b - qr + WIN_ROWS - 1 if rs <= ws + b < rs + WIN_ROWS else n_rel)
                if tuple(rel) not in pairs:
                    pairs.append(tuple(rel))
                ids.append(pairs.index(tuple(rel)))
    slabs = jnp.stack([jnp.concatenate([table[:, left], table[:, right]], axis=-1) for left, right in pairs], axis=1)
    return slabs, jnp.asarray(ids, jnp.int32)


def _gqa_kernel(q_ref, k_ref, vt_ref, gate_ref, o_ref, vt_ext_ref, *scratch, bounded):
    blocks, qgroup, _ = q_ref.shape
    units = blocks * (qgroup // B_TQ)
    n_chunks = SEQ // B_TKC

    @pl.when(pl.program_id(2) == 0)
    def _():
        for hh in range(2):
            vt_ext_ref[hh, :HEAD_DIM] = vt_ref[0, hh * HEAD_DIM:(hh + 1) * HEAD_DIM, :]
            vt_ext_ref[hh, HEAD_DIM:] = jnp.ones((LANES - HEAD_DIM, SEQ), BF16)

    masks = _head_masks()

    def rows(n):
        start = (n // blocks) * B_TQ
        return pl.ds(start if isinstance(n, int) else pl.multiple_of(start, B_TQ), B_TQ)

    def emit(n, ot):
        g = gate_ref[n % blocks, rows(n), :].astype(F32)
        o_ref[n % blocks, rows(n), :] = (ot.T * _silu(g)).astype(BF16)

    def masked_q(n):
        q2 = q_ref[n % blocks, rows(n), :]
        return [q2 * masks[hh] for hh in range(2)]

    def vt_chunk(hh, c):
        return vt_ext_ref[hh, :, c * B_TKC:(c + 1) * B_TKC]

    if bounded:
        def body(t, carry):
            items = [(t * B_UNROLL + u, c) for u in range(B_UNROLL) for c in range(n_chunks)]
            qm, st, accs = {}, {}, {}

            def qk(idx):
                n, c = items[idx]
                if c == 0:
                    qm[idx // n_chunks] = masked_q(n)
                for hh in range(2):
                    st[idx, hh] = _dot_nt(k_ref[0, c * B_TKC:(c + 1) * B_TKC, :], qm[idx // n_chunks][hh])

            qk(0)
            for idx, (n, c) in enumerate(items):
                if idx + 1 < len(items):
                    qk(idx + 1)
                for hh in range(2):
                    pv = _dot(vt_chunk(hh, c), jnp.exp2(st.pop((idx, hh))).astype(BF16))
                    accs[hh] = pv if c == 0 else accs[hh] + pv
                if c == n_chunks - 1:
                    emit(n, _normalise([accs[0], accs[1]]))
            return carry

        lax.fori_loop(0, units // B_UNROLL, body, 0)
        return

    st_ref, m_ref = scratch

    def scores(n, slot):
        qm = masked_q(n)
        return _score_steps(st_ref, m_ref, slot, B_TKC, n_chunks,
                            lambda hh, keys: _dot_nt(k_ref[0, keys, :], qm[hh]))

    def finish(n, slot):
        return _softmax_pv_steps(st_ref, m_ref, slot, B_TKC, n_chunks, vt_chunk, functools.partial(emit, n))

    _pipeline(units, st_ref.shape[0], B_AHEAD, scores, finish)


def _gqa(q, k, vt, gate, bounded):
    bsz = k.shape[0]
    blocks_per_pair = HEAD_PAIRS // (KV_HEADS_B // 2)
    groups = SEQ // B_QGROUP
    q_spec = pl.BlockSpec((blocks_per_pair, B_QGROUP, LANES), lambda b, j, g: (j, b * groups + g, 0))
    scratch = [pltpu.VMEM((2, LANES, SEQ), BF16)]
    if not bounded:
        scratch += [pltpu.VMEM((B_SLOTS, 2, SEQ, B_TQ), F32), pltpu.VMEM((B_SLOTS, 2, 1, B_TQ), F32)]
    return pl.pallas_call(
        functools.partial(_gqa_kernel, bounded=bounded),
        grid=(bsz, KV_HEADS_B // 2, groups),
        in_specs=[q_spec,
                  pl.BlockSpec((1, SEQ, LANES), lambda b, j, g: (b, 0, j)),
                  pl.BlockSpec((1, LANES, SEQ), lambda b, j, g: (b, j, 0)),
                  q_spec],
        out_specs=q_spec,
        out_shape=jax.ShapeDtypeStruct(q.shape, BF16),
        scratch_shapes=scratch,
        compiler_params=pltpu.CompilerParams(
            dimension_semantics=("parallel", "parallel", "arbitrary"),
            vmem_limit_bytes=VMEM_LIMIT),
        name="gqa_bounded" if bounded else "gqa",
    )(q, k, vt, gate)


def _pair_heads(w, axis):
    lead, trail = w.shape[:axis], w.shape[axis + 1:]
    group = HEADS // KV_HEADS_B
    w = w.reshape(*lead, KV_HEADS_B // 2, 2, group, HEAD_DIM, *trail)
    w = jnp.swapaxes(w, len(lead) + 1, len(lead) + 2)
    return w.reshape(*lead, D_MODEL, *trail)


def _rope_tables():
    t = np.arange(SEQ)
    pos = np.stack([t // GRID_W, t % GRID_W], axis=1).astype(np.float64)
    lane = np.arange(LANES) % HEAD_DIM
    section = lane // ROPE_SECTION
    n_freq = ROPE_SECTION // 2
    inv = np.power(ROPE_THETA, -np.arange(n_freq, dtype=np.float64) * 2.0 / ROPE_SECTION)
    ang = pos[:, section] * inv[lane % n_freq][None, :]
    sign = np.where(lane % ROPE_SECTION < n_freq, -1.0, 1.0)
    return jnp.asarray(np.cos(ang), F32), jnp.asarray(np.sin(ang) * sign[None, :], F32)


def kernel(x, p, norm_g, a_w_in, a_rpb, a_w_out, b_w_in, b_q_norm, b_k_norm, b_w_out, ple_norm_g, ple_w_gate,
           ple_w_proj, final_norm_g):
    bsz, seq, d = x.shape
    assert (seq, d) == (SEQ, D_MODEL)
    depth = p.shape[0]
    m = bsz * seq
    scale = math.log2(math.e) / math.sqrt(HEAD_DIM)
    xf = x.reshape(m, d)
    p_all = p.reshape(depth, m, PLE_DIM)
    cos, sin = _rope_tables()
    seg4 = np.kron(np.eye(4), np.ones((HEAD_DIM, HEAD_DIM)))
    seg = jnp.asarray(np.concatenate([seg4, seg4], axis=0), BF16)
    q_col_scale = np.ones((4 * D_MODEL,), np.float32)
    q_col_scale[:D_MODEL] = scale
    row2 = lambda v: v.reshape(1, -1)

    for i in range(depth):
        j = i // NUM_MIXERS
        if i % NUM_MIXERS == 0:
            w = (a_w_in[j] * q_col_scale).astype(BF16)
            q, k, vt, gate, n2 = _inproj_a(xf, row2(norm_g[i]), w, jnp.asarray(seg4, BF16))
            q, k, gate = (t.reshape(bsz, seq, D_MODEL) for t in (q, k, gate))
            rpb = a_rpb[j] * math.log2(math.e)
            slabs, slab_ids = _natten_bias(rpb)
            n2 = jnp.max(n2, axis=0)
            score_bound = 1.02 * jnp.sqrt(jnp.max(n2[0] * n2[1])) + jnp.max(jnp.abs(rpb))
            a = lax.cond(score_bound <= SCORE_BOUND,
                         lambda: _natten(q, k, vt, gate, slabs, slab_ids, bounded=True),
                         lambda: _natten(q, k, vt, gate, slabs, slab_ids, bounded=False)).reshape(m, D_MODEL)
            wo = a_w_out[j].astype(BF16)
        else:
            w = b_w_in[j]
            qk_w = D_MODEL + KV_W_B
            w = jnp.concatenate([_pair_heads(w[:, :D_MODEL], 1), w[:, D_MODEL:qk_w + KV_W_B],
                                 _pair_heads(w[:, qk_w + KV_W_B:], 1)], axis=1).astype(BF16)
            head_gain = jnp.concatenate([jnp.tile(b_q_norm[j] * scale, HEADS), jnp.tile(b_k_norm[j], KV_HEADS_B)])
            q, k, vt, gate = _inproj_b(xf, row2(norm_g[i]), w, row2(head_gain), cos, sin, seg)
            score_bound = HEAD_DIM * jnp.max(jnp.abs(b_q_norm[j] * scale)) * jnp.max(jnp.abs(b_k_norm[j]))
            k3 = k.reshape(bsz, seq, KV_W_B)
            a = lax.cond(score_bound <= SCORE_BOUND,
                         lambda: _gqa(q, k3, vt, gate, bounded=True),
                         lambda: _gqa(q, k3, vt, gate, bounded=False))
            wo = _pair_heads(b_w_out[j], 0).astype(BF16)
        xf = _outproj(a, xf, p_all, i, wo, row2(ple_norm_g[i]), ple_w_gate[i].astype(BF16),
                      ple_w_proj[i].astype(BF16), row2(final_norm_g), final_norm=(i == depth - 1))
    return xf.reshape(bsz, seq, d)
```

```python
import functools
import math

import jax
import jax.numpy as jnp
import numpy as np
from jax import lax
from jax.experimental import pallas as pl
from jax.experimental.pallas import tpu as pltpu

D_MODEL = 1024
SEQ = 4096
GRID_W = 64
GRID_ROWS = SEQ // GRID_W
HEAD_DIM = 64
HEADS = D_MODEL // HEAD_DIM
KV_HEADS_B = 4
KV_W_B = KV_HEADS_B * HEAD_DIM
WIN_ROWS = 8
WIN_COLS = 16
ROPE_THETA = 10000.0
ROPE_SECTION = HEAD_DIM // 2
PLE_DIM = 256
NORM_EPS = 1e-6
NUM_MIXERS = 2

LANES = 128
HEAD_PAIRS = HEADS // 2
NEG_BIAS = -1e30

ROW_TILE = 1024
A_QROWS = 4
A_KROWS = 12
A_KCHUNK = 4
A_TQ = A_QROWS * GRID_W
A_TK = A_KROWS * GRID_W
A_TKC = A_KCHUNK * GRID_W
A_NKC = A_KROWS // A_KCHUNK
A_BATCH = 2
A_UNROLL = 16
A_SLOTS, A_AHEAD = 4, 2
B_SLOTS, B_AHEAD = 2, 1
B_UNROLL = 8
SCORE_BOUND = 50.0
VALUE_BOUND = 2.0 ** 60
NORM_ROW_GROUPS = 4
B_TQ = 256
B_QGROUP = 4096
B_TKC = 4096
VMEM_LIMIT = 56 * 1024 * 1024

BF16 = jnp.bfloat16
F32 = jnp.float32


def _rms(x, g):
    ms = jnp.mean(x * x, axis=-1, keepdims=True)
    return x * lax.rsqrt(ms + NORM_EPS) * g


def _dot(a, b):
    return jnp.dot(a, b, preferred_element_type=F32)


def _dot_nt(a, b):
    return lax.dot_general(a, b, (((1,), (1,)), ((), ())), preferred_element_type=F32)


def _head_masks():
    lo = (lax.broadcasted_iota(jnp.int32, (1, LANES), 1) < HEAD_DIM).astype(BF16)
    return lo, 1 - lo


def _silu(g):
    return g * jax.nn.sigmoid(g)


def _score_steps(st_ref, m_ref, slot, key_chunk, n_chunks, chunk_scores):
    maxes = ([], [])

    def chunk(c):
        keys = pl.ds(c * key_chunk, key_chunk)
        for hh in range(2):
            st = chunk_scores(hh, keys)
            st_ref[slot, hh, keys, :] = st
            maxes[hh].append(jnp.max(st, axis=0, keepdims=True))

    def done():
        for hh in range(2):
            m_ref[slot, hh] = functools.reduce(jnp.maximum, maxes[hh])

    return [functools.partial(chunk, c) for c in range(n_chunks)] + [done]


def _normalise(accs):
    return jnp.concatenate([a[:HEAD_DIM] / a[HEAD_DIM:HEAD_DIM + 1] for a in accs], axis=0)


def _softmax_pv_steps(st_ref, m_ref, slot, key_chunk, n_chunks, vt_chunk, emit):
    accs = [None, None]

    def chunk(c):
        keys = pl.ds(c * key_chunk, key_chunk)
        for hh in range(2):
            pt = jnp.exp2(st_ref[slot, hh, keys, :] - m_ref[slot, hh]).astype(BF16)
            pv = _dot(vt_chunk(hh, c), pt)
            accs[hh] = pv if accs[hh] is None else accs[hh] + pv

    def done():
        emit(_normalise(accs))

    return [functools.partial(chunk, c) for c in range(n_chunks)] + [done]


def _pipeline(units, slots, ahead, scores, finish):
    assert units % slots == 0 and 0 < ahead < slots

    def run(*step_lists):
        for steps in zip(*step_lists):
            for step in reversed(steps):
                step()

    def trip(base, last):
        for k in range(slots):
            if not last or k + ahead < slots:
                run(scores(base + k + ahead, (k + ahead) % slots), finish(base + k, k))
            else:
                run(finish(base + k, k))

    for n in range(ahead):
        run(scores(n, n))

    def body(t, carry):
        trip(t * slots, False)
        return carry

    lax.fori_loop(0, units // slots - 1, body, 0)
    trip(units - slots, True)


def _inproj_a_kernel(x_ref, g_ref, w_ref, seg_ref, q_ref, k_ref, vt_ref, gate_ref, n2_ref):
    h = _rms(x_ref[...], g_ref[...]).astype(BF16)
    wide = 2 * LANES
    head_norms = []
    for c, o_ref in enumerate((q_ref, k_ref)):
        y = _dot(h, w_ref[:, c * D_MODEL:(c + 1) * D_MODEL])
        o_ref[...] = y.astype(BF16)
        ss = y * y
        rows = ss.shape[0] // NORM_ROW_GROUPS
        ss = functools.reduce(jnp.maximum, [ss[r * rows:(r + 1) * rows] for r in range(NORM_ROW_GROUPS)]).astype(BF16)
        n2 = jnp.concatenate([_dot(ss[:, b * wide:(b + 1) * wide], seg_ref[...]) for b in range(D_MODEL // wide)],
                             axis=1)
        head_norms.append(jnp.max(n2, axis=0, keepdims=True))
    gate_ref[...] = _dot(h, w_ref[:, 3 * D_MODEL:]).astype(BF16)
    vt = lax.dot_general(w_ref[:, 2 * D_MODEL:3 * D_MODEL], h, (((0,), (1,)), ((), ())),
                         preferred_element_type=F32)
    v_max = jnp.max(jnp.max(jnp.abs(vt), axis=0, keepdims=True), axis=1, keepdims=True)
    n2_ref[0] = jnp.concatenate(head_norms + [jnp.broadcast_to(v_max, (1, D_MODEL))], axis=0)
    vt = vt.astype(BF16)
    for c in range(vt_ref.shape[1]):
        vt_ref[0, c] = vt[:, c * A_TKC:(c + 1) * A_TKC]


def _inproj_a(x, g, w, seg):
    m = x.shape[0]
    out = jax.ShapeDtypeStruct((m, D_MODEL), BF16)
    tiles = SEQ // ROW_TILE
    fixed = lambda i: (0, 0)
    row_spec = pl.BlockSpec((ROW_TILE, D_MODEL), lambda i: (i, 0))
    return pl.pallas_call(
        _inproj_a_kernel,
        grid=(m // ROW_TILE,),
        in_specs=[row_spec,
                  pl.BlockSpec((1, D_MODEL), fixed),
                  pl.BlockSpec((D_MODEL, 4 * D_MODEL), fixed),
                  pl.BlockSpec((2 * LANES, 2 * LANES), fixed)],
        out_specs=[row_spec, row_spec,
                   pl.BlockSpec((1, ROW_TILE // A_TKC, D_MODEL, A_TKC), lambda i: (i // tiles, i % tiles, 0, 0)),
                   row_spec,
                   pl.BlockSpec((1, 3, D_MODEL), lambda i: (i, 0, 0))],
        out_shape=[out, out, jax.ShapeDtypeStruct((m // SEQ, SEQ // A_TKC, D_MODEL, A_TKC), BF16), out,
                   jax.ShapeDtypeStruct((m // ROW_TILE, 3, D_MODEL), F32)],
        compiler_params=pltpu.CompilerParams(dimension_semantics=("parallel",), vmem_limit_bytes=VMEM_LIMIT),
        name="inproj_a",
    )(x, g, w, seg)


def _inproj_b_kernel(x_ref, g_ref, w_ref, hg_ref, cos_ref, sin_ref, seg_ref, q_ref, k_ref, vt_ref, gate_ref,
                     vmax_ref):
    qk_w = D_MODEL + KV_W_B
    h = _rms(x_ref[...], g_ref[...]).astype(BF16)
    seg = seg_ref[...]
    cos = cos_ref[...]
    sin = sin_ref[...]
    first_half = (lax.broadcasted_iota(jnp.int32, (1, LANES), 1) % ROPE_SECTION) < (ROPE_SECTION // 2)
    yqk = _dot(h, w_ref[:, :qk_w])
    wide = 2 * LANES
    for pb in range(qk_w // wide):
        y = yqk[:, pb * wide:(pb + 1) * wide]
        ss = y * y
        hi = ss.astype(BF16)
        lo = (ss - hi.astype(F32)).astype(BF16)
        ms = _dot(jnp.concatenate([hi, lo], axis=1), seg) * (1.0 / HEAD_DIM)
        yn2 = y * lax.rsqrt(ms + NORM_EPS) * hg_ref[:, pb * wide:(pb + 1) * wide]
        for half in range(2):
            c = 2 * pb + half
            yn = yn2[:, half * LANES:(half + 1) * LANES]
            partner = jnp.where(first_half, pltpu.roll(yn, LANES - ROPE_SECTION // 2, 1),
                                pltpu.roll(yn, ROPE_SECTION // 2, 1))
            r = (yn * cos + partner * sin).astype(BF16)
            if c < D_MODEL // LANES:
                q_ref[c] = r
            else:
                k_ref[:, (c - D_MODEL // LANES) * LANES:(c - D_MODEL // LANES + 1) * LANES] = r
    vt = lax.dot_general(w_ref[:, qk_w:qk_w + KV_W_B], h, (((0,), (1,)), ((), ())),
                         preferred_element_type=F32)
    vt_ref[0] = vt.astype(BF16)
    v_max = jnp.max(jnp.max(jnp.abs(vt), axis=0, keepdims=True), axis=1, keepdims=True)
    vmax_ref[0] = jnp.broadcast_to(v_max, (1, LANES))
    gate = _dot(h, w_ref[:, qk_w + KV_W_B:])
    for c in range(D_MODEL // LANES):
        gate_ref[c] = gate[:, c * LANES:(c + 1) * LANES].astype(BF16)


def _inproj_b(x, g, w, head_gain, cos, sin, seg):
    m = x.shape[0]
    n_w = 2 * D_MODEL + 2 * KV_W_B
    tiles = SEQ // ROW_TILE
    row = lambda i: (i, 0)
    fixed = lambda i: (0, 0)
    pos = lambda i: (i % tiles, 0)
    wide = jax.ShapeDtypeStruct((D_MODEL // LANES, m, LANES), BF16)
    wide_spec = pl.BlockSpec((D_MODEL // LANES, ROW_TILE, LANES), lambda i: (0, i, 0))
    return pl.pallas_call(
        _inproj_b_kernel,
        grid=(m // ROW_TILE,),
        in_specs=[pl.BlockSpec((ROW_TILE, D_MODEL), row),
                  pl.BlockSpec((1, D_MODEL), fixed),
                  pl.BlockSpec((D_MODEL, n_w), fixed),
                  pl.BlockSpec((1, D_MODEL + KV_W_B), fixed),
                  pl.BlockSpec((ROW_TILE, LANES), pos),
                  pl.BlockSpec((ROW_TILE, LANES), pos),
                  pl.BlockSpec((4 * LANES, 2 * LANES), fixed)],
        out_specs=[wide_spec,
                   pl.BlockSpec((ROW_TILE, KV_W_B), row),
                   pl.BlockSpec((1, KV_W_B, ROW_TILE), lambda i: (i // tiles, 0, i % tiles)),
                   wide_spec,
                   pl.BlockSpec((1, 1, LANES), lambda i: (i, 0, 0))],
        out_shape=[wide, jax.ShapeDtypeStruct((m, KV_W_B), BF16),
                   jax.ShapeDtypeStruct((m // SEQ, KV_W_B, SEQ), BF16), wide,
                   jax.ShapeDtypeStruct((m // ROW_TILE, 1, LANES), F32)],
        compiler_params=pltpu.CompilerParams(dimension_semantics=("parallel",), vmem_limit_bytes=VMEM_LIMIT),
        name="inproj_b",
    )(x, g, w, head_gain, cos, sin, seg)


def _outproj_kernel(a_ref, x_ref, p_ref, wo_ref, pg_ref, wg_ref, wp_ref, fg_ref, o_ref, *, final_norm):
    if len(a_ref.shape) == 3:
        a = jnp.concatenate([a_ref[c] for c in range(a_ref.shape[0])], axis=1)
    else:
        a = a_ref[...]
    x1 = x_ref[...] + _dot(a, wo_ref[...])
    hn = _rms(x1, pg_ref[...]).astype(BF16)
    gate = jax.nn.sigmoid(_dot(hn, wg_ref[...]))
    x2 = x1 + gate * _dot(p_ref[...].astype(BF16), wp_ref[...])
    o_ref[...] = _rms(x2, fg_ref[...]) if final_norm else x2


def _outproj(a, x, p_all, layer, wo, pg, wg, wp, fg, final_norm):
    m = x.shape[0]
    row = lambda i: (i, 0)
    fixed = lambda i: (0, 0)
    return pl.pallas_call(
        functools.partial(_outproj_kernel, final_norm=final_norm),
        grid=(m // ROW_TILE,),
        in_specs=[pl.BlockSpec((ROW_TILE, D_MODEL), row) if a.ndim == 2 else
                  pl.BlockSpec((a.shape[0], ROW_TILE, LANES), lambda i: (0, i, 0)),
                  pl.BlockSpec((ROW_TILE, D_MODEL), row),
                  pl.BlockSpec((None, ROW_TILE, PLE_DIM), lambda i: (layer, i, 0)),
                  pl.BlockSpec((D_MODEL, D_MODEL), fixed),
                  pl.BlockSpec((1, D_MODEL), fixed),
                  pl.BlockSpec((D_MODEL, D_MODEL), fixed),
                  pl.BlockSpec((PLE_DIM, D_MODEL), fixed),
                  pl.BlockSpec((1, D_MODEL), fixed)],
        out_specs=pl.BlockSpec((ROW_TILE, D_MODEL), row),
        out_shape=jax.ShapeDtypeStruct((m, D_MODEL), F32),
        compiler_params=pltpu.CompilerParams(dimension_semantics=("parallel",), vmem_limit_bytes=VMEM_LIMIT),
        name="outproj_final" if final_norm else "outproj",
    )(a, x, p_all, wo, pg, wg, wp, fg)


def _natten_kernel(slab_ids_ref, q_ref, k_ref, vt_ref, gate_ref, bias_ref, o_ref, *scratch, bounded):
    n_blocks = GRID_ROWS // A_QROWS
    masks = _head_masks()
    ones = jnp.ones((LANES - HEAD_DIM, A_TK), BF16)

    def where(n):
        n = jnp.asarray(n, jnp.int32)
        bb, i = n // n_blocks, n % n_blocks
        first_chunk = jnp.clip(i - 1, 0, n_blocks - A_NKC)
        variant = jnp.where(i == 0, 0, jnp.where(i == n_blocks - 1, 2, 1))
        return bb, pl.ds(pl.multiple_of(i * A_TQ, A_TQ), A_TQ), first_chunk, variant

    def unit_scores(n):
        bb, q_rows, first_chunk, variant = where(n)
        q2 = q_ref[bb, q_rows, :]
        kwin = k_ref[bb, pl.ds(pl.multiple_of(first_chunk * A_TKC, A_TKC), A_TK), :]
        out = []
        for hh in range(2):
            bias = jnp.concatenate(
                [jnp.concatenate([bias_ref[hh, slab_ids_ref[(variant * A_KROWS + r) * 2 + half]]
                                  for half in range(2)], axis=1) for r in range(A_KROWS)], axis=0)
            out.append(_dot_nt(kwin, q2 * masks[hh]) + bias)
        return out

    def vt_ext(n):
        bb, _, first_chunk, _ = where(n)
        vt = jnp.concatenate([vt_ref[bb, first_chunk + kc] for kc in range(A_NKC)], axis=1)
        return [jnp.concatenate([vt[hh * HEAD_DIM:(hh + 1) * HEAD_DIM], ones], axis=0) for hh in range(2)]

    def emit(n, ot):
        bb, q_rows, _, _ = where(n)
        o_ref[bb, q_rows, :] = (ot.T * _silu(gate_ref[bb, q_rows, :].astype(F32))).astype(BF16)

    units = q_ref.shape[0] * n_blocks
    if bounded:
        def body(t, carry):
            st = {0: unit_scores(t * A_UNROLL)}
            for u in range(A_UNROLL):
                n = t * A_UNROLL + u
                if u + 1 < A_UNROLL:
                    st[u + 1] = unit_scores(n + 1)
                vt = vt_ext(n)
                emit(n, _normalise([_dot(v, jnp.exp2(sc).astype(BF16)) for v, sc in zip(vt, st.pop(u))]))
            return carry

        lax.fori_loop(0, units // A_UNROLL, body, 0)
        return

    st_ref, m_ref = scratch

    def scores(n, slot):
        sc = unit_scores(n)
        return _score_steps(st_ref, m_ref, slot, A_TK, 1, lambda hh, keys: sc[hh])

    def finish(n, slot):
        vt = vt_ext(n)
        return _softmax_pv_steps(st_ref, m_ref, slot, A_TK, 1, lambda hh, c: vt[hh], functools.partial(emit, n))

    _pipeline(units, st_ref.shape[0], A_AHEAD, scores, finish)


def _natten(q, k, vt, gate, slabs, slab_ids, bounded):
    bsz = q.shape[0]
    tok_spec = pl.BlockSpec((A_BATCH, SEQ, LANES), lambda p, g: (g, 0, p))
    scratch = [] if bounded else [pltpu.VMEM((A_SLOTS, 2, A_TK, A_TQ), F32),
                                  pltpu.VMEM((A_SLOTS, 2, 1, A_TQ), F32)]
    return pl.pallas_call(
        functools.partial(_natten_kernel, bounded=bounded),
        grid=(HEAD_PAIRS, bsz // A_BATCH),
        in_specs=[pl.BlockSpec(memory_space=pltpu.SMEM),
                  tok_spec, tok_spec,
                  pl.BlockSpec((A_BATCH, SEQ // A_TKC, LANES, A_TKC), lambda p, g: (g, 0, p, 0)),
                  tok_spec,
                  pl.BlockSpec((2,) + slabs.shape[1:], lambda p, g: (p, 0, 0, 0))],
        out_specs=tok_spec,
        out_shape=jax.ShapeDtypeStruct(q.shape, BF16),
        scratch_shapes=scratch,
        compiler_params=pltpu.CompilerParams(dimension_semantics=("parallel", "parallel"),
                                             vmem_limit_bytes=VMEM_LIMIT),
        name="natten_bounded" if bounded else "natten",
    )(slab_ids, q, k, vt, gate, slabs)


def _natten_bias(rpb):
    qc = np.arange(GRID_W)
    kc = np.arange(GRID_W)
    cs = np.clip(qc - WIN_COLS // 2, 0, GRID_W - WIN_COLS)
    col_ok = (kc[:, None] >= cs[None, :]) & (kc[:, None] < cs[None, :] + WIN_COLS)
    n_rel, n_col = 2 * WIN_ROWS - 1, 2 * WIN_COLS - 1
    dc = np.clip(kc[:, None] - qc[None, :] + WIN_COLS - 1, 0, n_col - 1)
    select = (np.arange(n_col)[:, None] == dc.reshape(1, -1)).astype(np.float32)
    table = jnp.dot(rpb.reshape(HEADS * n_rel, n_col), select, precision=lax.Precision.HIGHEST)
    table = jnp.where(col_ok[None, None], table.reshape(HEADS, n_rel, GRID_W, GRID_W), NEG_BIAS)
    table = jnp.concatenate([table, jnp.full((HEADS, 1, GRID_W, GRID_W), NEG_BIAS, F32)], axis=1)
    n_blocks = GRID_ROWS // A_QROWS
    pairs, ids = [], []
    for i in (0, 1, n_blocks - 1):
        ws = A_KCHUNK * min(max(i - 1, 0), n_blocks - A_NKC)
        for b in range(A_KROWS):
            for half in range(A_QROWS // 2):
                rel = []
                for a in (2 * half, 2 * half + 1):
                    qr = A_QROWS * i + a
                    rs = min(max(qr - WIN_ROWS // 2, 0), GRID_ROWS - WIN_ROWS)
                    rel.append(ws + b - qr + WIN_ROWS - 1 if rs <= ws + b < rs + WIN_ROWS else n_rel)
                if tuple(rel) not in pairs:
                    pairs.append(tuple(rel))
                ids.append(pairs.index(tuple(rel)))
    slabs = jnp.stack([jnp.concatenate([table[:, left], table[:, right]], axis=-1) for left, right in pairs], axis=1)
    return slabs, jnp.asarray(ids, jnp.int32)


def _gqa_kernel(q_ref, k_ref, vt_ref, gate_ref, o_ref, vt_ext_ref, *scratch, bounded):
    blocks, qgroup, _ = q_ref.shape
    units = blocks * (qgroup // B_TQ)
    n_chunks = SEQ // B_TKC

    @pl.when(pl.program_id(2) == 0)
    def _():
        for hh in range(2):
            vt_ext_ref[hh, :HEAD_DIM] = vt_ref[0, hh * HEAD_DIM:(hh + 1) * HEAD_DIM, :]
            vt_ext_ref[hh, HEAD_DIM:] = jnp.ones((LANES - HEAD_DIM, SEQ), BF16)

    masks = _head_masks()

    def rows(n):
        start = (n // blocks) * B_TQ
        return pl.ds(start if isinstance(n, int) else pl.multiple_of(start, B_TQ), B_TQ)

    def emit(n, ot):
        g = gate_ref[n % blocks, rows(n), :].astype(F32)
        o_ref[n % blocks, rows(n), :] = (ot.T * _silu(g)).astype(BF16)

    def masked_q(n):
        q2 = q_ref[n % blocks, rows(n), :]
        return [q2 * masks[hh] for hh in range(2)]

    def vt_chunk(hh, c):
        return vt_ext_ref[hh, :, c * B_TKC:(c + 1) * B_TKC]

    if bounded:
        def body(t, carry):
            items = [(t * B_UNROLL + u, c) for u in range(B_UNROLL) for c in range(n_chunks)]
            qm, st, accs = {}, {}, {}

            def qk(idx):
                n, c = items[idx]
                if c == 0:
                    qm[idx // n_chunks] = masked_q(n)
                for hh in range(2):
                    st[idx, hh] = _dot_nt(k_ref[0, c * B_TKC:(c + 1) * B_TKC, :], qm[idx // n_chunks][hh])

            qk(0)
            for idx, (n, c) in enumerate(items):
                if idx + 1 < len(items):
                    qk(idx + 1)
                for hh in range(2):
                    pv = _dot(vt_chunk(hh, c), jnp.exp2(st.pop((idx, hh))).astype(BF16))
                    accs[hh] = pv if c == 0 else accs[hh] + pv
                if c == n_chunks - 1:
                    emit(n, _normalise([accs[0], accs[1]]))
            return carry

        lax.fori_loop(0, units // B_UNROLL, body, 0)
        return

    st_ref, m_ref = scratch

    def scores(n, slot):
        qm = masked_q(n)
        return _score_steps(st_ref, m_ref, slot, B_TKC, n_chunks,
                            lambda hh, keys: _dot_nt(k_ref[0, keys, :], qm[hh]))

    def finish(n, slot):
        return _softmax_pv_steps(st_ref, m_ref, slot, B_TKC, n_chunks, vt_chunk, functools.partial(emit, n))

    _pipeline(units, st_ref.shape[0], B_AHEAD, scores, finish)


def _gqa(q, k, vt, gate, bounded):
    bsz = k.shape[0]
    blocks_per_pair = HEAD_PAIRS // (KV_HEADS_B // 2)
    groups = SEQ // B_QGROUP
    q_spec = pl.BlockSpec((blocks_per_pair, B_QGROUP, LANES), lambda b, j, g: (j, b * groups + g, 0))
    scratch = [pltpu.VMEM((2, LANES, SEQ), BF16)]
    if not bounded:
        scratch += [pltpu.VMEM((B_SLOTS, 2, SEQ, B_TQ), F32), pltpu.VMEM((B_SLOTS, 2, 1, B_TQ), F32)]
    return pl.pallas_call(
        functools.partial(_gqa_kernel, bounded=bounded),
        grid=(bsz, KV_HEADS_B // 2, groups),
        in_specs=[q_spec,
                  pl.BlockSpec((1, SEQ, LANES), lambda b, j, g: (b, 0, j)),
                  pl.BlockSpec((1, LANES, SEQ), lambda b, j, g: (b, j, 0)),
                  q_spec],
        out_specs=q_spec,
        out_shape=jax.ShapeDtypeStruct(q.shape, BF16),
        scratch_shapes=scratch,
        compiler_params=pltpu.CompilerParams(
            dimension_semantics=("parallel", "parallel", "arbitrary"),
            vmem_limit_bytes=VMEM_LIMIT),
        name="gqa_bounded" if bounded else "gqa",
    )(q, k, vt, gate)


def _pair_heads(w, axis):
    lead, trail = w.shape[:axis], w.shape[axis + 1:]
    group = HEADS // KV_HEADS_B
    w = w.reshape(*lead, KV_HEADS_B // 2, 2, group, HEAD_DIM, *trail)
    w = jnp.swapaxes(w, len(lead) + 1, len(lead) + 2)
    return w.reshape(*lead, D_MODEL, *trail)


def _rope_tables():
    t = np.arange(SEQ)
    pos = np.stack([t // GRID_W, t % GRID_W], axis=1).astype(np.float64)
    lane = np.arange(LANES) % HEAD_DIM
    section = lane // ROPE_SECTION
    n_freq = ROPE_SECTION // 2
    inv = np.power(ROPE_THETA, -np.arange(n_freq, dtype=np.float64) * 2.0 / ROPE_SECTION)
    ang = pos[:, section] * inv[lane % n_freq][None, :]
    sign = np.where(lane % ROPE_SECTION < n_freq, -1.0, 1.0)
    return jnp.asarray(np.cos(ang), F32), jnp.asarray(np.sin(ang) * sign[None, :], F32)


def kernel(x, p, norm_g, a_w_in, a_rpb, a_w_out, b_w_in, b_q_norm, b_k_norm, b_w_out, ple_norm_g, ple_w_gate,
           ple_w_proj, final_norm_g):
    bsz, seq, d = x.shape
    assert (seq, d) == (SEQ, D_MODEL)
    depth = p.shape[0]
    m = bsz * seq
    scale = math.log2(math.e) / math.sqrt(HEAD_DIM)
    xf = x.reshape(m, d)
    p_all = p.reshape(depth, m, PLE_DIM)
    cos, sin = _rope_tables()
    seg4 = np.kron(np.eye(4), np.ones((HEAD_DIM, HEAD_DIM)))
    seg = jnp.asarray(np.concatenate([seg4, seg4], axis=0), BF16)
    q_col_scale = np.ones((4 * D_MODEL,), np.float32)
    q_col_scale[:D_MODEL] = scale
    row2 = lambda v: v.reshape(1, -1)

    for i in range(depth):
        j = i // NUM_MIXERS
        if i % NUM_MIXERS == 0:
            w = (a_w_in[j] * q_col_scale).astype(BF16)
            q, k, vt, gate, n2 = _inproj_a(xf, row2(norm_g[i]), w, jnp.asarray(seg4, BF16))
            q, k, gate = (t.reshape(bsz, seq, D_MODEL) for t in (q, k, gate))
            rpb = a_rpb[j] * math.log2(math.e)
            slabs, slab_ids = _natten_bias(rpb)
            n2 = jnp.max(n2, axis=0)
            score_bound = 1.02 * jnp.sqrt(jnp.max(n2[0] * n2[1])) + jnp.max(jnp.abs(rpb))
            a = lax.cond((score_bound <= SCORE_BOUND) & (jnp.max(n2[2]) <= VALUE_BOUND),
                         lambda: _natten(q, k, vt, gate, slabs, slab_ids, bounded=True),
                         lambda: _natten(q, k, vt, gate, slabs, slab_ids, bounded=False)).reshape(m, D_MODEL)
            wo = a_w_out[j].astype(BF16)
        else:
            w = b_w_in[j]
            qk_w = D_MODEL + KV_W_B
            w = jnp.concatenate([_pair_heads(w[:, :D_MODEL], 1), w[:, D_MODEL:qk_w + KV_W_B],
                                 _pair_heads(w[:, qk_w + KV_W_B:], 1)], axis=1).astype(BF16)
            head_gain = jnp.concatenate([jnp.tile(b_q_norm[j] * scale, HEADS), jnp.tile(b_k_norm[j], KV_HEADS_B)])
            q, k, vt, gate, v_max = _inproj_b(xf, row2(norm_g[i]), w, row2(head_gain), cos, sin, seg)
            score_bound = HEAD_DIM * jnp.max(jnp.abs(b_q_norm[j] * scale)) * jnp.max(jnp.abs(b_k_norm[j]))
            k3 = k.reshape(bsz, seq, KV_W_B)
            a = lax.cond((score_bound <= SCORE_BOUND) & (jnp.max(v_max) <= VALUE_BOUND),
                         lambda: _gqa(q, k3, vt, gate, bounded=True),
                         lambda: _gqa(q, k3, vt, gate, bounded=False))
            wo = _pair_heads(b_w_out[j], 0).astype(BF16)
        xf = _outproj(a, xf, p_all, i, wo, row2(ple_norm_g[i]), ple_w_gate[i].astype(BF16),
                      ple_w_proj[i].astype(BF16), row2(final_norm_g), final_norm=(i == depth - 1))
    return xf.reshape(bsz, seq, d)
```

```python
import functools
import math

import jax
import jax.numpy as jnp
import numpy as np
from jax import lax
from jax.experimental import pallas as pl
from jax.experimental.pallas import tpu as pltpu

D_MODEL = 1024
SEQ = 4096
GRID_W = 64
GRID_ROWS = SEQ // GRID_W
HEAD_DIM = 64
HEADS = D_MODEL // HEAD_DIM
KV_HEADS_B = 4
KV_W_B = KV_HEADS_B * HEAD_DIM
WIN_ROWS = 8
WIN_COLS = 16
ROPE_THETA = 10000.0
ROPE_SECTION = HEAD_DIM // 2
PLE_DIM = 256
NORM_EPS = 1e-6
NUM_MIXERS = 2

LANES = 128
HEAD_PAIRS = HEADS // 2
NEG_BIAS = -1e30

ROW_TILE = 1024
A_QROWS = 4
A_KROWS = 12
A_KCHUNK = 4
A_TQ = A_QROWS * GRID_W
A_TK = A_KROWS * GRID_W
A_TKC = A_KCHUNK * GRID_W
A_NKC = A_KROWS // A_KCHUNK
A_BATCH = 2
A_UNROLL = 16
A_SLOTS, A_AHEAD = 4, 2
B_SLOTS, B_AHEAD = 2, 1
B_UNROLL = 8
SCORE_BOUND = 80.0
VALUE_BOUND = 2.0 ** 30
NORM_ROW_GROUPS = 4
B_TQ = 256
B_QGROUP = 4096
B_TKC = 4096
VMEM_LIMIT = 56 * 1024 * 1024

BF16 = jnp.bfloat16
F32 = jnp.float32


def _rms(x, g):
    ms = jnp.mean(x * x, axis=-1, keepdims=True)
    return x * lax.rsqrt(ms + NORM_EPS) * g


def _dot(a, b):
    return jnp.dot(a, b, preferred_element_type=F32)


def _dot_nt(a, b):
    return lax.dot_general(a, b, (((1,), (1,)), ((), ())), preferred_element_type=F32)


def _head_masks():
    lo = (lax.broadcasted_iota(jnp.int32, (1, LANES), 1) < HEAD_DIM).astype(BF16)
    return lo, 1 - lo


def _silu(g):
    return g * jax.nn.sigmoid(g)


def _score_steps(st_ref, m_ref, slot, key_chunk, n_chunks, chunk_scores):
    maxes = ([], [])

    def chunk(c):
        keys = pl.ds(c * key_chunk, key_chunk)
        for hh in range(2):
            st = chunk_scores(hh, keys)
            st_ref[slot, hh, keys, :] = st
            maxes[hh].append(jnp.max(st, axis=0, keepdims=True))

    def done():
        for hh in range(2):
            m_ref[slot, hh] = functools.reduce(jnp.maximum, maxes[hh])

    return [functools.partial(chunk, c) for c in range(n_chunks)] + [done]


def _normalise(accs):
    return jnp.concatenate([a[:HEAD_DIM] / a[HEAD_DIM:HEAD_DIM + 1] for a in accs], axis=0)


def _softmax_pv_steps(st_ref, m_ref, slot, key_chunk, n_chunks, vt_chunk, emit):
    accs = [None, None]

    def chunk(c):
        keys = pl.ds(c * key_chunk, key_chunk)
        for hh in range(2):
            pt = jnp.exp2(st_ref[slot, hh, keys, :] - m_ref[slot, hh]).astype(BF16)
            pv = _dot(vt_chunk(hh, c), pt)
            accs[hh] = pv if accs[hh] is None else accs[hh] + pv

    def done():
        emit(_normalise(accs))

    return [functools.partial(chunk, c) for c in range(n_chunks)] + [done]


def _pipeline(units, slots, ahead, scores, finish):
    assert units % slots == 0 and 0 < ahead < slots

    def run(*step_lists):
        for steps in zip(*step_lists):
            for step in reversed(steps):
                step()

    def trip(base, last):
        for k in range(slots):
            if not last or k + ahead < slots:
                run(scores(base + k + ahead, (k + ahead) % slots), finish(base + k, k))
            else:
                run(finish(base + k, k))

    for n in range(ahead):
        run(scores(n, n))

    def body(t, carry):
        trip(t * slots, False)
        return carry

    lax.fori_loop(0, units // slots - 1, body, 0)
    trip(units - slots, True)


def _inproj_a_kernel(x_ref, g_ref, w_ref, seg_ref, q_ref, k_ref, vt_ref, gate_ref, n2_ref):
    h = _rms(x_ref[...], g_ref[...]).astype(BF16)
    wide = 2 * LANES
    head_norms = []
    for c, o_ref in enumerate((q_ref, k_ref)):
        y = _dot(h, w_ref[:, c * D_MODEL:(c + 1) * D_MODEL])
        o_ref[...] = y.astype(BF16)
        ss = y * y
        rows = ss.shape[0] // NORM_ROW_GROUPS
        ss = functools.reduce(jnp.maximum, [ss[r * rows:(r + 1) * rows] for r in range(NORM_ROW_GROUPS)]).astype(BF16)
        n2 = jnp.concatenate([_dot(ss[:, b * wide:(b + 1) * wide], seg_ref[...]) for b in range(D_MODEL // wide)],
                             axis=1)
        head_norms.append(jnp.max(n2, axis=0, keepdims=True))
    gate_ref[...] = _dot(h, w_ref[:, 3 * D_MODEL:]).astype(BF16)
    vt = lax.dot_general(w_ref[:, 2 * D_MODEL:3 * D_MODEL], h, (((0,), (1,)), ((), ())),
                         preferred_element_type=F32)
    v_max = jnp.max(jnp.max(jnp.abs(vt), axis=0, keepdims=True), axis=1, keepdims=True)
    n2_ref[0] = jnp.concatenate(head_norms + [jnp.broadcast_to(v_max, (1, D_MODEL))], axis=0)
    vt = vt.astype(BF16)
    for c in range(vt_ref.shape[1]):
        vt_ref[0, c] = vt[:, c * A_TKC:(c + 1) * A_TKC]


def _inproj_a(x, g, w, seg):
    m = x.shape[0]
    out = jax.ShapeDtypeStruct((m, D_MODEL), BF16)
    tiles = SEQ // ROW_TILE
    fixed = lambda i: (0, 0)
    row_spec = pl.BlockSpec((ROW_TILE, D_MODEL), lambda i: (i, 0))
    return pl.pallas_call(
        _inproj_a_kernel,
        grid=(m // ROW_TILE,),
        in_specs=[row_spec,
                  pl.BlockSpec((1, D_MODEL), fixed),
                  pl.BlockSpec((D_MODEL, 4 * D_MODEL), fixed),
                  pl.BlockSpec((2 * LANES, 2 * LANES), fixed)],
        out_specs=[row_spec, row_spec,
                   pl.BlockSpec((1, ROW_TILE // A_TKC, D_MODEL, A_TKC), lambda i: (i // tiles, i % tiles, 0, 0)),
                   row_spec,
                   pl.BlockSpec((1, 3, D_MODEL), lambda i: (i, 0, 0))],
        out_shape=[out, out, jax.ShapeDtypeStruct((m // SEQ, SEQ // A_TKC, D_MODEL, A_TKC), BF16), out,
                   jax.ShapeDtypeStruct((m // ROW_TILE, 3, D_MODEL), F32)],
        compiler_params=pltpu.CompilerParams(dimension_semantics=("parallel",), vmem_limit_bytes=VMEM_LIMIT),
        name="inproj_a",
    )(x, g, w, seg)


def _inproj_b_kernel(x_ref, g_ref, w_ref, hg_ref, cos_ref, sin_ref, seg_ref, q_ref, k_ref, vt_ref, gate_ref,
                     vmax_ref):
    qk_w = D_MODEL + KV_W_B
    h = _rms(x_ref[...], g_ref[...]).astype(BF16)
    seg = seg_ref[...]
    cos = cos_ref[...]
    sin = sin_ref[...]
    first_half = (lax.broadcasted_iota(jnp.int32, (1, LANES), 1) % ROPE_SECTION) < (ROPE_SECTION // 2)
    yqk = _dot(h, w_ref[:, :qk_w])
    wide = 2 * LANES
    for pb in range(qk_w // wide):
        y = yqk[:, pb * wide:(pb + 1) * wide]
        ss = y * y
        hi = ss.astype(BF16)
        lo = (ss - hi.astype(F32)).astype(BF16)
        ms = _dot(jnp.concatenate([hi, lo], axis=1), seg) * (1.0 / HEAD_DIM)
        yn2 = y * lax.rsqrt(ms + NORM_EPS) * hg_ref[:, pb * wide:(pb + 1) * wide]
        for half in range(2):
            c = 2 * pb + half
            yn = yn2[:, half * LANES:(half + 1) * LANES]
            partner = jnp.where(first_half, pltpu.roll(yn, LANES - ROPE_SECTION // 2, 1),
                                pltpu.roll(yn, ROPE_SECTION // 2, 1))
            r = (yn * cos + partner * sin).astype(BF16)
            if c < D_MODEL // LANES:
                q_ref[c] = r
            else:
                k_ref[:, (c - D_MODEL // LANES) * LANES:(c - D_MODEL // LANES + 1) * LANES] = r
    vt = lax.dot_general(w_ref[:, qk_w:qk_w + KV_W_B], h, (((0,), (1,)), ((), ())),
                         preferred_element_type=F32)
    vt_ref[0] = vt.astype(BF16)
    v_max = jnp.max(jnp.max(jnp.abs(vt), axis=0, keepdims=True), axis=1, keepdims=True)
    vmax_ref[0] = jnp.broadcast_to(v_max, (1, LANES))
    gate = _dot(h, w_ref[:, qk_w + KV_W_B:])
    for c in range(D_MODEL // LANES):
        gate_ref[c] = gate[:, c * LANES:(c + 1) * LANES].astype(BF16)


def _inproj_b(x, g, w, head_gain, cos, sin, seg):
    m = x.shape[0]
    n_w = 2 * D_MODEL + 2 * KV_W_B
    tiles = SEQ // ROW_TILE
    row = lambda i: (i, 0)
    fixed = lambda i: (0, 0)
    pos = lambda i: (i % tiles, 0)
    wide = jax.ShapeDtypeStruct((D_MODEL // LANES, m, LANES), BF16)
    wide_spec = pl.BlockSpec((D_MODEL // LANES, ROW_TILE, LANES), lambda i: (0, i, 0))
    return pl.pallas_call(
        _inproj_b_kernel,
        grid=(m // ROW_TILE,),
        in_specs=[pl.BlockSpec((ROW_TILE, D_MODEL), row),
                  pl.BlockSpec((1, D_MODEL), fixed),
                  pl.BlockSpec((D_MODEL, n_w), fixed),
                  pl.BlockSpec((1, D_MODEL + KV_W_B), fixed),
                  pl.BlockSpec((ROW_TILE, LANES), pos),
                  pl.BlockSpec((ROW_TILE, LANES), pos),
                  pl.BlockSpec((4 * LANES, 2 * LANES), fixed)],
        out_specs=[wide_spec,
                   pl.BlockSpec((ROW_TILE, KV_W_B), row),
                   pl.BlockSpec((1, KV_W_B, ROW_TILE), lambda i: (i // tiles, 0, i % tiles)),
                   wide_spec,
                   pl.BlockSpec((1, 1, LANES), lambda i: (i, 0, 0))],
        out_shape=[wide, jax.ShapeDtypeStruct((m, KV_W_B), BF16),
                   jax.ShapeDtypeStruct((m // SEQ, KV_W_B, SEQ), BF16), wide,
                   jax.ShapeDtypeStruct((m // ROW_TILE, 1, LANES), F32)],
        compiler_params=pltpu.CompilerParams(dimension_semantics=("parallel",), vmem_limit_bytes=VMEM_LIMIT),
        name="inproj_b",
    )(x, g, w, head_gain, cos, sin, seg)


def _outproj_kernel(a_ref, x_ref, p_ref, wo_ref, pg_ref, wg_ref, wp_ref, fg_ref, o_ref, *, final_norm):
    if len(a_ref.shape) == 3:
        a = jnp.concatenate([a_ref[c] for c in range(a_ref.shape[0])], axis=1)
    else:
        a = a_ref[...]
    x1 = x_ref[...] + _dot(a, wo_ref[...])
    hn = _rms(x1, pg_ref[...]).astype(BF16)
    gate = jax.nn.sigmoid(_dot(hn, wg_ref[...]))
    x2 = x1 + gate * _dot(p_ref[...].astype(BF16), wp_ref[...])
    o_ref[...] = _rms(x2, fg_ref[...]) if final_norm else x2


def _outproj(a, x, p_all, layer, wo, pg, wg, wp, fg, final_norm):
    m = x.shape[0]
    row = lambda i: (i, 0)
    fixed = lambda i: (0, 0)
    return pl.pallas_call(
        functools.partial(_outproj_kernel, final_norm=final_norm),
        grid=(m // ROW_TILE,),
        in_specs=[pl.BlockSpec((ROW_TILE, D_MODEL), row) if a.ndim == 2 else
                  pl.BlockSpec((a.shape[0], ROW_TILE, LANES), lambda i: (0, i, 0)),
                  pl.BlockSpec((ROW_TILE, D_MODEL), row),
                  pl.BlockSpec((None, ROW_TILE, PLE_DIM), lambda i: (layer, i, 0)),
                  pl.BlockSpec((D_MODEL, D_MODEL), fixed),
                  pl.BlockSpec((1, D_MODEL), fixed),
                  pl.BlockSpec((D_MODEL, D_MODEL), fixed),
                  pl.BlockSpec((PLE_DIM, D_MODEL), fixed),
                  pl.BlockSpec((1, D_MODEL), fixed)],
        out_specs=pl.BlockSpec((ROW_TILE, D_MODEL), row),
        out_shape=jax.ShapeDtypeStruct((m, D_MODEL), F32),
        compiler_params=pltpu.CompilerParams(dimension_semantics=("parallel",), vmem_limit_bytes=VMEM_LIMIT),
        name="outproj_final" if final_norm else "outproj",
    )(a, x, p_all, wo, pg, wg, wp, fg)


def _natten_kernel(slab_ids_ref, q_ref, k_ref, vt_ref, gate_ref, bias_ref, o_ref, *scratch, bounded):
    n_blocks = GRID_ROWS // A_QROWS
    masks = _head_masks()
    ones = jnp.ones((LANES - HEAD_DIM, A_TK), BF16)

    def where(n):
        n = jnp.asarray(n, jnp.int32)
        bb, i = n // n_blocks, n % n_blocks
        first_chunk = jnp.clip(i - 1, 0, n_blocks - A_NKC)
        variant = jnp.where(i == 0, 0, jnp.where(i == n_blocks - 1, 2, 1))
        return bb, pl.ds(pl.multiple_of(i * A_TQ, A_TQ), A_TQ), first_chunk, variant

    def unit_scores(n):
        bb, q_rows, first_chunk, variant = where(n)
        q2 = q_ref[bb, q_rows, :]
        kwin = k_ref[bb, pl.ds(pl.multiple_of(first_chunk * A_TKC, A_TKC), A_TK), :]
        out = []
        for hh in range(2):
            bias = jnp.concatenate(
                [jnp.concatenate([bias_ref[hh, slab_ids_ref[(variant * A_KROWS + r) * 2 + half]]
                                  for half in range(2)], axis=1) for r in range(A_KROWS)], axis=0)
            out.append(_dot_nt(kwin, q2 * masks[hh]) + bias)
        return out

    def vt_ext(n):
        bb, _, first_chunk, _ = where(n)
        vt = jnp.concatenate([vt_ref[bb, first_chunk + kc] for kc in range(A_NKC)], axis=1)
        return [jnp.concatenate([vt[hh * HEAD_DIM:(hh + 1) * HEAD_DIM], ones], axis=0) for hh in range(2)]

    def emit(n, ot):
        bb, q_rows, _, _ = where(n)
        o_ref[bb, q_rows, :] = (ot.T * _silu(gate_ref[bb, q_rows, :].astype(F32))).astype(BF16)

    units = q_ref.shape[0] * n_blocks
    if bounded:
        def body(t, carry):
            st = {0: unit_scores(t * A_UNROLL)}
            for u in range(A_UNROLL):
                n = t * A_UNROLL + u
                if u + 1 < A_UNROLL:
                    st[u + 1] = unit_scores(n + 1)
                vt = vt_ext(n)
                emit(n, _normalise([_dot(v, jnp.exp2(sc).astype(BF16)) for v, sc in zip(vt, st.pop(u))]))
            return carry

        lax.fori_loop(0, units // A_UNROLL, body, 0)
        return

    st_ref, m_ref = scratch

    def scores(n, slot):
        sc = unit_scores(n)
        return _score_steps(st_ref, m_ref, slot, A_TK, 1, lambda hh, keys: sc[hh])

    def finish(n, slot):
        vt = vt_ext(n)
        return _softmax_pv_steps(st_ref, m_ref, slot, A_TK, 1, lambda hh, c: vt[hh], functools.partial(emit, n))

    _pipeline(units, st_ref.shape[0], A_AHEAD, scores, finish)


def _natten(q, k, vt, gate, slabs, slab_ids, bounded):
    bsz = q.shape[0]
    tok_spec = pl.BlockSpec((A_BATCH, SEQ, LANES), lambda p, g: (g, 0, p))
    scratch = [] if bounded else [pltpu.VMEM((A_SLOTS, 2, A_TK, A_TQ), F32),
                                  pltpu.VMEM((A_SLOTS, 2, 1, A_TQ), F32)]
    return pl.pallas_call(
        functools.partial(_natten_kernel, bounded=bounded),
        grid=(HEAD_PAIRS, bsz // A_BATCH),
        in_specs=[pl.BlockSpec(memory_space=pltpu.SMEM),
                  tok_spec, tok_spec,
                  pl.BlockSpec((A_BATCH, SEQ // A_TKC, LANES, A_TKC), lambda p, g: (g, 0, p, 0)),
                  tok_spec,
                  pl.BlockSpec((2,) + slabs.shape[1:], lambda p, g: (p, 0, 0, 0))],
        out_specs=tok_spec,
        out_shape=jax.ShapeDtypeStruct(q.shape, BF16),
        scratch_shapes=scratch,
        compiler_params=pltpu.CompilerParams(dimension_semantics=("parallel", "parallel"),
                                             vmem_limit_bytes=VMEM_LIMIT),
        name="natten_bounded" if bounded else "natten",
    )(slab_ids, q, k, vt, gate, slabs)


def _natten_bias(rpb):
    qc = np.arange(GRID_W)
    kc = np.arange(GRID_W)
    cs = np.clip(qc - WIN_COLS // 2, 0, GRID_W - WIN_COLS)
    col_ok = (kc[:, None] >= cs[None, :]) & (kc[:, None] < cs[None, :] + WIN_COLS)
    n_rel, n_col = 2 * WIN_ROWS - 1, 2 * WIN_COLS - 1
    dc = np.clip(kc[:, None] - qc[None, :] + WIN_COLS - 1, 0, n_col - 1)
    select = (np.arange(n_col)[:, None] == dc.reshape(1, -1)).astype(np.float32)
    table = jnp.dot(rpb.reshape(HEADS * n_rel, n_col), select, precision=lax.Precision.HIGHEST)
    table = jnp.where(col_ok[None, None], table.reshape(HEADS, n_rel, GRID_W, GRID_W), NEG_BIAS)
    table = jnp.concatenate([table, jnp.full((HEADS, 1, GRID_W, GRID_W), NEG_BIAS, F32)], axis=1)
    n_blocks = GRID_ROWS // A_QROWS
    pairs, ids = [], []
    for i in (0, 1, n_blocks - 1):
        ws = A_KCHUNK * min(max(i - 1, 0), n_blocks - A_NKC)
        for b in range(A_KROWS):
            for half in range(A_QROWS // 2):
                rel = []
                for a in (2 * half, 2 * half + 1):
                    qr = A_QROWS * i + a
                    rs = min(max(qr - WIN_ROWS // 2, 0), GRID_ROWS - WIN_ROWS)
                    rel.append(ws + b - qr + WIN_ROWS - 1 if rs <= ws + b < rs + WIN_ROWS else n_rel)
                if tuple(rel) not in pairs:
                    pairs.append(tuple(rel))
                ids.append(pairs.index(tuple(rel)))
    slabs = jnp.stack([jnp.concatenate([table[:, left], table[:, right]], axis=-1) for left, right in pairs], axis=1)
    return slabs, jnp.asarray(ids, jnp.int32)


def _gqa_kernel(q_ref, k_ref, vt_ref, gate_ref, o_ref, vt_ext_ref, *scratch, bounded):
    blocks, qgroup, _ = q_ref.shape
    units = blocks * (qgroup // B_TQ)
    n_chunks = SEQ // B_TKC

    @pl.when(pl.program_id(2) == 0)
    def _():
        for hh in range(2):
            vt_ext_ref[hh, :HEAD_DIM] = vt_ref[0, hh * HEAD_DIM:(hh + 1) * HEAD_DIM, :]
            vt_ext_ref[hh, HEAD_DIM:] = jnp.ones((LANES - HEAD_DIM, SEQ), BF16)

    masks = _head_masks()

    def rows(n):
        start = (n // blocks) * B_TQ
        return pl.ds(start if isinstance(n, int) else pl.multiple_of(start, B_TQ), B_TQ)

    def emit(n, ot):
        g = gate_ref[n % blocks, rows(n), :].astype(F32)
        o_ref[n % blocks, rows(n), :] = (ot.T * _silu(g)).astype(BF16)

    def masked_q(n):
        q2 = q_ref[n % blocks, rows(n), :]
        return [q2 * masks[hh] for hh in range(2)]

    def vt_chunk(hh, c):
        return vt_ext_ref[hh, :, c * B_TKC:(c + 1) * B_TKC]

    if bounded:
        def body(t, carry):
            items = [(t * B_UNROLL + u, c) for u in range(B_UNROLL) for c in range(n_chunks)]
            qm, st, accs = {}, {}, {}

            def qk(idx):
                n, c = items[idx]
                if c == 0:
                    qm[idx // n_chunks] = masked_q(n)
                for hh in range(2):
                    st[idx, hh] = _dot_nt(k_ref[0, c * B_TKC:(c + 1) * B_TKC, :], qm[idx // n_chunks][hh])

            qk(0)
            for idx, (n, c) in enumerate(items):
                if idx + 1 < len(items):
                    qk(idx + 1)
                for hh in range(2):
                    pv = _dot(vt_chunk(hh, c), jnp.exp2(st.pop((idx, hh))).astype(BF16))
                    accs[hh] = pv if c == 0 else accs[hh] + pv
                if c == n_chunks - 1:
                    emit(n, _normalise([accs[0], accs[1]]))
            return carry

        lax.fori_loop(0, units // B_UNROLL, body, 0)
        return

    st_ref, m_ref = scratch

    def scores(n, slot):
        qm = masked_q(n)
        return _score_steps(st_ref, m_ref, slot, B_TKC, n_chunks,
                            lambda hh, keys: _dot_nt(k_ref[0, keys, :], qm[hh]))

    def finish(n, slot):
        return _softmax_pv_steps(st_ref, m_ref, slot, B_TKC, n_chunks, vt_chunk, functools.partial(emit, n))

    _pipeline(units, st_ref.shape[0], B_AHEAD, scores, finish)


def _gqa(q, k, vt, gate, bounded):
    bsz = k.shape[0]
    blocks_per_pair = HEAD_PAIRS // (KV_HEADS_B // 2)
    groups = SEQ // B_QGROUP
    q_spec = pl.BlockSpec((blocks_per_pair, B_QGROUP, LANES), lambda b, j, g: (j, b * groups + g, 0))
    scratch = [pltpu.VMEM((2, LANES, SEQ), BF16)]
    if not bounded:
        scratch += [pltpu.VMEM((B_SLOTS, 2, SEQ, B_TQ), F32), pltpu.VMEM((B_SLOTS, 2, 1, B_TQ), F32)]
    return pl.pallas_call(
        functools.partial(_gqa_kernel, bounded=bounded),
        grid=(bsz, KV_HEADS_B // 2, groups),
        in_specs=[q_spec,
                  pl.BlockSpec((1, SEQ, LANES), lambda b, j, g: (b, 0, j)),
                  pl.BlockSpec((1, LANES, SEQ), lambda b, j, g: (b, j, 0)),
                  q_spec],
        out_specs=q_spec,
        out_shape=jax.ShapeDtypeStruct(q.shape, BF16),
        scratch_shapes=scratch,
        compiler_params=pltpu.CompilerParams(
            dimension_semantics=("parallel", "parallel", "arbitrary"),
            vmem_limit_bytes=VMEM_LIMIT),
        name="gqa_bounded" if bounded else "gqa",
    )(q, k, vt, gate)


def _pair_heads(w, axis):
    lead, trail = w.shape[:axis], w.shape[axis + 1:]
    group = HEADS // KV_HEADS_B
    w = w.reshape(*lead, KV_HEADS_B // 2, 2, group, HEAD_DIM, *trail)
    w = jnp.swapaxes(w, len(lead) + 1, len(lead) + 2)
    return w.reshape(*lead, D_MODEL, *trail)


def _rope_tables():
    t = np.arange(SEQ)
    pos = np.stack([t // GRID_W, t % GRID_W], axis=1).astype(np.float64)
    lane = np.arange(LANES) % HEAD_DIM
    section = lane // ROPE_SECTION
    n_freq = ROPE_SECTION // 2
    inv = np.power(ROPE_THETA, -np.arange(n_freq, dtype=np.float64) * 2.0 / ROPE_SECTION)
    ang = pos[:, section] * inv[lane % n_freq][None, :]
    sign = np.where(lane % ROPE_SECTION < n_freq, -1.0, 1.0)
    return jnp.asarray(np.cos(ang), F32), jnp.asarray(np.sin(ang) * sign[None, :], F32)


def kernel(x, p, norm_g, a_w_in, a_rpb, a_w_out, b_w_in, b_q_norm, b_k_norm, b_w_out, ple_norm_g, ple_w_gate,
           ple_w_proj, final_norm_g):
    bsz, seq, d = x.shape
    assert (seq, d) == (SEQ, D_MODEL)
    depth = p.shape[0]
    m = bsz * seq
    scale = math.log2(math.e) / math.sqrt(HEAD_DIM)
    xf = x.reshape(m, d)
    p_all = p.reshape(depth, m, PLE_DIM)
    cos, sin = _rope_tables()
    seg4 = np.kron(np.eye(4), np.ones((HEAD_DIM, HEAD_DIM)))
    seg = jnp.asarray(np.concatenate([seg4, seg4], axis=0), BF16)
    q_col_scale = np.ones((4 * D_MODEL,), np.float32)
    q_col_scale[:D_MODEL] = scale
    row2 = lambda v: v.reshape(1, -1)

    for i in range(depth):
        j = i // NUM_MIXERS
        if i % NUM_MIXERS == 0:
            w = (a_w_in[j] * q_col_scale).astype(BF16)
            q, k, vt, gate, n2 = _inproj_a(xf, row2(norm_g[i]), w, jnp.asarray(seg4, BF16))
            q, k, gate = (t.reshape(bsz, seq, D_MODEL) for t in (q, k, gate))
            rpb = a_rpb[j] * math.log2(math.e)
            slabs, slab_ids = _natten_bias(rpb)
            n2 = jnp.max(n2, axis=0)
            score_bound = 1.02 * jnp.sqrt(jnp.max(n2[0] * n2[1])) + jnp.max(jnp.abs(rpb))
            a = lax.cond((score_bound <= SCORE_BOUND) & (jnp.max(n2[2]) <= VALUE_BOUND),
                         lambda: _natten(q, k, vt, gate, slabs, slab_ids, bounded=True),
                         lambda: _natten(q, k, vt, gate, slabs, slab_ids, bounded=False)).reshape(m, D_MODEL)
            wo = a_w_out[j].astype(BF16)
        else:
            w = b_w_in[j]
            qk_w = D_MODEL + KV_W_B
            w = jnp.concatenate([_pair_heads(w[:, :D_MODEL], 1), w[:, D_MODEL:qk_w + KV_W_B],
                                 _pair_heads(w[:, qk_w + KV_W_B:], 1)], axis=1).astype(BF16)
            head_gain = jnp.concatenate([jnp.tile(b_q_norm[j] * scale, HEADS), jnp.tile(b_k_norm[j], KV_HEADS_B)])
            q, k, vt, gate, v_max = _inproj_b(xf, row2(norm_g[i]), w, row2(head_gain), cos, sin, seg)
            score_bound = HEAD_DIM * jnp.max(jnp.abs(b_q_norm[j] * scale)) * jnp.max(jnp.abs(b_k_norm[j]))
            k3 = k.reshape(bsz, seq, KV_W_B)
            a = lax.cond((score_bound <= SCORE_BOUND) & (jnp.max(v_max) <= VALUE_BOUND),
                         lambda: _gqa(q, k3, vt, gate, bounded=True),
                         lambda: _gqa(q, k3, vt, gate, bounded=False))
            wo = _pair_heads(b_w_out[j], 0).astype(BF16)
        xf = _outproj(a, xf, p_all, i, wo, row2(ple_norm_g[i]), ple_w_gate[i].astype(BF16),
                      ple_w_proj[i].astype(BF16), row2(final_norm_g), final_norm=(i == depth - 1))
    return xf.reshape(bsz, seq, d)
```

```python
import functools
import math

import jax
import jax.numpy as jnp
import numpy as np
from jax import lax
from jax.experimental import pallas as pl
from jax.experimental.pallas import tpu as pltpu

D_MODEL = 1024
SEQ = 4096
GRID_W = 64
GRID_ROWS = SEQ // GRID_W
HEAD_DIM = 64
HEADS = D_MODEL // HEAD_DIM
KV_HEADS_B = 4
KV_W_B = KV_HEADS_B * HEAD_DIM
WIN_ROWS = 8
WIN_COLS = 16
ROPE_THETA = 10000.0
ROPE_SECTION = HEAD_DIM // 2
PLE_DIM = 256
NORM_EPS = 1e-6
NUM_MIXERS = 2

LANES = 128
HEAD_PAIRS = HEADS // 2
NEG_BIAS = -1e30

ROW_TILE = 1024
A_QROWS = 4
A_KROWS = 12
A_KCHUNK = 4
A_TQ = A_QROWS * GRID_W
A_TK = A_KROWS * GRID_W
A_TKC = A_KCHUNK * GRID_W
A_NKC = A_KROWS // A_KCHUNK
A_BATCH = 4
A_UNROLL = 16
A_SLOTS, A_AHEAD = 4, 2
B_SLOTS, B_AHEAD = 2, 1
B_UNROLL = 16
SCORE_BOUND = 80.0
VALUE_BOUND = 2.0 ** 30
NORM_ROW_GROUPS = 4
B_TQ = 256
B_QGROUP = 4096
B_TKC = 4096
VMEM_LIMIT = 56 * 1024 * 1024

BF16 = jnp.bfloat16
F32 = jnp.float32


def _rms(x, g):
    ms = jnp.mean(x * x, axis=-1, keepdims=True)
    return x * lax.rsqrt(ms + NORM_EPS) * g


def _dot(a, b):
    return jnp.dot(a, b, preferred_element_type=F32)


def _dot_nt(a, b):
    return lax.dot_general(a, b, (((1,), (1,)), ((), ())), preferred_element_type=F32)


def _head_masks():
    lo = (lax.broadcasted_iota(jnp.int32, (1, LANES), 1) < HEAD_DIM).astype(BF16)
    return lo, 1 - lo


def _silu(g):
    return g * jax.nn.sigmoid(g)


def _score_steps(st_ref, m_ref, slot, key_chunk, n_chunks, chunk_scores):
    maxes = ([], [])

    def chunk(c):
        keys = pl.ds(c * key_chunk, key_chunk)
        for hh in range(2):
            st = chunk_scores(hh, keys)
            st_ref[slot, hh, keys, :] = st
            maxes[hh].append(jnp.max(st, axis=0, keepdims=True))

    def done():
        for hh in range(2):
            m_ref[slot, hh] = functools.reduce(jnp.maximum, maxes[hh])

    return [functools.partial(chunk, c) for c in range(n_chunks)] + [done]


def _normalise(accs):
    return jnp.concatenate([a[:HEAD_DIM] / a[HEAD_DIM:HEAD_DIM + 1] for a in accs], axis=0)


def _softmax_pv_steps(st_ref, m_ref, slot, key_chunk, n_chunks, vt_chunk, emit):
    accs = [None, None]

    def chunk(c):
        keys = pl.ds(c * key_chunk, key_chunk)
        for hh in range(2):
            pt = jnp.exp2(st_ref[slot, hh, keys, :] - m_ref[slot, hh]).astype(BF16)
            pv = _dot(vt_chunk(hh, c), pt)
            accs[hh] = pv if accs[hh] is None else accs[hh] + pv

    def done():
        emit(_normalise(accs))

    return [functools.partial(chunk, c) for c in range(n_chunks)] + [done]


def _pipeline(units, slots, ahead, scores, finish):
    assert units % slots == 0 and 0 < ahead < slots

    def run(*step_lists):
        for steps in zip(*step_lists):
            for step in reversed(steps):
                step()

    def trip(base, last):
        for k in range(slots):
            if not last or k + ahead < slots:
                run(scores(base + k + ahead, (k + ahead) % slots), finish(base + k, k))
            else:
                run(finish(base + k, k))

    for n in range(ahead):
        run(scores(n, n))

    def body(t, carry):
        trip(t * slots, False)
        return carry

    lax.fori_loop(0, units // slots - 1, body, 0)
    trip(units - slots, True)


def _inproj_a_kernel(x_ref, g_ref, w_ref, seg_ref, q_ref, k_ref, vt_ref, gate_ref, n2_ref):
    h = _rms(x_ref[...], g_ref[...]).astype(BF16)
    wide = 2 * LANES
    head_norms = []
    for c, o_ref in enumerate((q_ref, k_ref)):
        y = _dot(h, w_ref[:, c * D_MODEL:(c + 1) * D_MODEL])
        o_ref[...] = y.astype(BF16)
        ss = y * y
        rows = ss.shape[0] // NORM_ROW_GROUPS
        ss = functools.reduce(jnp.maximum, [ss[r * rows:(r + 1) * rows] for r in range(NORM_ROW_GROUPS)]).astype(BF16)
        n2 = jnp.concatenate([_dot(ss[:, b * wide:(b + 1) * wide], seg_ref[...]) for b in range(D_MODEL // wide)],
                             axis=1)
        head_norms.append(jnp.max(n2, axis=0, keepdims=True))
    gate_ref[...] = _dot(h, w_ref[:, 3 * D_MODEL:]).astype(BF16)
    vt = lax.dot_general(w_ref[:, 2 * D_MODEL:3 * D_MODEL], h, (((0,), (1,)), ((), ())),
                         preferred_element_type=F32)
    v_max = jnp.max(jnp.max(jnp.abs(vt), axis=0, keepdims=True), axis=1, keepdims=True)
    n2_ref[0] = jnp.concatenate(head_norms + [jnp.broadcast_to(v_max, (1, D_MODEL))], axis=0)
    vt = vt.astype(BF16)
    for c in range(vt_ref.shape[1]):
        vt_ref[0, c] = vt[:, c * A_TKC:(c + 1) * A_TKC]


def _inproj_a(x, g, w, seg):
    m = x.shape[0]
    out = jax.ShapeDtypeStruct((m, D_MODEL), BF16)
    tiles = SEQ // ROW_TILE
    fixed = lambda i: (0, 0)
    row_spec = pl.BlockSpec((ROW_TILE, D_MODEL), lambda i: (i, 0))
    return pl.pallas_call(
        _inproj_a_kernel,
        grid=(m // ROW_TILE,),
        in_specs=[row_spec,
                  pl.BlockSpec((1, D_MODEL), fixed),
                  pl.BlockSpec((D_MODEL, 4 * D_MODEL), fixed),
                  pl.BlockSpec((2 * LANES, 2 * LANES), fixed)],
        out_specs=[row_spec, row_spec,
                   pl.BlockSpec((1, ROW_TILE // A_TKC, D_MODEL, A_TKC), lambda i: (i // tiles, i % tiles, 0, 0)),
                   row_spec,
                   pl.BlockSpec((1, 3, D_MODEL), lambda i: (i, 0, 0))],
        out_shape=[out, out, jax.ShapeDtypeStruct((m // SEQ, SEQ // A_TKC, D_MODEL, A_TKC), BF16), out,
                   jax.ShapeDtypeStruct((m // ROW_TILE, 3, D_MODEL), F32)],
        compiler_params=pltpu.CompilerParams(dimension_semantics=("parallel",), vmem_limit_bytes=VMEM_LIMIT),
        name="inproj_a",
    )(x, g, w, seg)


def _inproj_b_kernel(x_ref, g_ref, w_ref, hg_ref, cos_ref, sin_ref, seg_ref, q_ref, k_ref, vt_ref, gate_ref,
                     vmax_ref):
    qk_w = D_MODEL + KV_W_B
    h = _rms(x_ref[...], g_ref[...]).astype(BF16)
    seg = seg_ref[...]
    cos = cos_ref[...]
    sin = sin_ref[...]
    first_half = (lax.broadcasted_iota(jnp.int32, (1, LANES), 1) % ROPE_SECTION) < (ROPE_SECTION // 2)
    yqk = _dot(h, w_ref[:, :qk_w])
    wide = 2 * LANES
    for pb in range(qk_w // wide):
        y = yqk[:, pb * wide:(pb + 1) * wide]
        ss = y * y
        hi = ss.astype(BF16)
        lo = (ss - hi.astype(F32)).astype(BF16)
        ms = _dot(jnp.concatenate([hi, lo], axis=1), seg) * (1.0 / HEAD_DIM)
        yn2 = y * lax.rsqrt(ms + NORM_EPS) * hg_ref[:, pb * wide:(pb + 1) * wide]
        for half in range(2):
            c = 2 * pb + half
            yn = yn2[:, half * LANES:(half + 1) * LANES]
            partner = jnp.where(first_half, pltpu.roll(yn, LANES - ROPE_SECTION // 2, 1),
                                pltpu.roll(yn, ROPE_SECTION // 2, 1))
            r = (yn * cos + partner * sin).astype(BF16)
            if c < D_MODEL // LANES:
                q_ref[c] = r
            else:
                k_ref[:, (c - D_MODEL // LANES) * LANES:(c - D_MODEL // LANES + 1) * LANES] = r
    vt = lax.dot_general(w_ref[:, qk_w:qk_w + KV_W_B], h, (((0,), (1,)), ((), ())),
                         preferred_element_type=F32)
    vt_ref[0] = vt.astype(BF16)
    v_max = jnp.max(jnp.max(jnp.abs(vt), axis=0, keepdims=True), axis=1, keepdims=True)
    vmax_ref[0] = jnp.broadcast_to(v_max, (1, LANES))
    gate = _dot(h, w_ref[:, qk_w + KV_W_B:])
    for c in range(D_MODEL // LANES):
        gate_ref[c] = gate[:, c * LANES:(c + 1) * LANES].astype(BF16)


def _inproj_b(x, g, w, head_gain, cos, sin, seg):
    m = x.shape[0]
    n_w = 2 * D_MODEL + 2 * KV_W_B
    tiles = SEQ // ROW_TILE
    row = lambda i: (i, 0)
    fixed = lambda i: (0, 0)
    pos = lambda i: (i % tiles, 0)
    wide = jax.ShapeDtypeStruct((D_MODEL // LANES, m, LANES), BF16)
    wide_spec = pl.BlockSpec((D_MODEL // LANES, ROW_TILE, LANES), lambda i: (0, i, 0))
    return pl.pallas_call(
        _inproj_b_kernel,
        grid=(m // ROW_TILE,),
        in_specs=[pl.BlockSpec((ROW_TILE, D_MODEL), row),
                  pl.BlockSpec((1, D_MODEL), fixed),
                  pl.BlockSpec((D_MODEL, n_w), fixed),
                  pl.BlockSpec((1, D_MODEL + KV_W_B), fixed),
                  pl.BlockSpec((ROW_TILE, LANES), pos),
                  pl.BlockSpec((ROW_TILE, LANES), pos),
                  pl.BlockSpec((4 * LANES, 2 * LANES), fixed)],
        out_specs=[wide_spec,
                   pl.BlockSpec((ROW_TILE, KV_W_B), row),
                   pl.BlockSpec((1, KV_W_B, ROW_TILE), lambda i: (i // tiles, 0, i % tiles)),
                   wide_spec,
                   pl.BlockSpec((1, 1, LANES), lambda i: (i, 0, 0))],
        out_shape=[wide, jax.ShapeDtypeStruct((m, KV_W_B), BF16),
                   jax.ShapeDtypeStruct((m // SEQ, KV_W_B, SEQ), BF16), wide,
                   jax.ShapeDtypeStruct((m // ROW_TILE, 1, LANES), F32)],
        compiler_params=pltpu.CompilerParams(dimension_semantics=("parallel",), vmem_limit_bytes=VMEM_LIMIT),
        name="inproj_b",
    )(x, g, w, head_gain, cos, sin, seg)


def _outproj_kernel(a_ref, x_ref, p_ref, wo_ref, pg_ref, wg_ref, wp_ref, fg_ref, o_ref, *, final_norm):
    if len(a_ref.shape) == 3:
        a = jnp.concatenate([a_ref[c] for c in range(a_ref.shape[0])], axis=1)
    else:
        a = a_ref[...]
    x1 = x_ref[...] + _dot(a, wo_ref[...])
    hn = _rms(x1, pg_ref[...]).astype(BF16)
    gate = jax.nn.sigmoid(_dot(hn, wg_ref[...]))
    x2 = x1 + gate * _dot(p_ref[...].astype(BF16), wp_ref[...])
    o_ref[...] = _rms(x2, fg_ref[...]) if final_norm else x2


def _outproj(a, x, p_all, layer, wo, pg, wg, wp, fg, final_norm):
    m = x.shape[0]
    row = lambda i: (i, 0)
    fixed = lambda i: (0, 0)
    return pl.pallas_call(
        functools.partial(_outproj_kernel, final_norm=final_norm),
        grid=(m // ROW_TILE,),
        in_specs=[pl.BlockSpec((ROW_TILE, D_MODEL), row) if a.ndim == 2 else
                  pl.BlockSpec((a.shape[0], ROW_TILE, LANES), lambda i: (0, i, 0)),
                  pl.BlockSpec((ROW_TILE, D_MODEL), row),
                  pl.BlockSpec((None, ROW_TILE, PLE_DIM), lambda i: (layer, i, 0)),
                  pl.BlockSpec((D_MODEL, D_MODEL), fixed),
                  pl.BlockSpec((1, D_MODEL), fixed),
                  pl.BlockSpec((D_MODEL, D_MODEL), fixed),
                  pl.BlockSpec((PLE_DIM, D_MODEL), fixed),
                  pl.BlockSpec((1, D_MODEL), fixed)],
        out_specs=pl.BlockSpec((ROW_TILE, D_MODEL), row),
        out_shape=jax.ShapeDtypeStruct((m, D_MODEL), F32),
        compiler_params=pltpu.CompilerParams(dimension_semantics=("parallel",), vmem_limit_bytes=VMEM_LIMIT),
        name="outproj_final" if final_norm else "outproj",
    )(a, x, p_all, wo, pg, wg, wp, fg)


def _natten_kernel(slab_ids_ref, q_ref, k_ref, vt_ref, gate_ref, bias_ref, o_ref, *scratch, bounded):
    n_blocks = GRID_ROWS // A_QROWS
    masks = _head_masks()
    ones = jnp.ones((LANES - HEAD_DIM, A_TK), BF16)

    def where(n):
        n = jnp.asarray(n, jnp.int32)
        bb, i = n // n_blocks, n % n_blocks
        first_chunk = jnp.clip(i - 1, 0, n_blocks - A_NKC)
        variant = jnp.where(i == 0, 0, jnp.where(i == n_blocks - 1, 2, 1))
        return bb, pl.ds(pl.multiple_of(i * A_TQ, A_TQ), A_TQ), first_chunk, variant

    def unit_scores(n):
        bb, q_rows, first_chunk, variant = where(n)
        q2 = q_ref[bb, q_rows, :]
        kwin = k_ref[bb, pl.ds(pl.multiple_of(first_chunk * A_TKC, A_TKC), A_TK), :]
        out = []
        for hh in range(2):
            bias = jnp.concatenate(
                [jnp.concatenate([bias_ref[hh, slab_ids_ref[(variant * A_KROWS + r) * 2 + half]]
                                  for half in range(2)], axis=1) for r in range(A_KROWS)], axis=0)
            out.append(_dot_nt(kwin, q2 * masks[hh]) + bias)
        return out

    def vt_ext(n):
        bb, _, first_chunk, _ = where(n)
        vt = jnp.concatenate([vt_ref[bb, first_chunk + kc] for kc in range(A_NKC)], axis=1)
        return [jnp.concatenate([vt[hh * HEAD_DIM:(hh + 1) * HEAD_DIM], ones], axis=0) for hh in range(2)]

    def emit(n, ot):
        bb, q_rows, _, _ = where(n)
        o_ref[bb, q_rows, :] = (ot.T * _silu(gate_ref[bb, q_rows, :].astype(F32))).astype(BF16)

    units = q_ref.shape[0] * n_blocks
    if bounded:
        def body(t, carry):
            st = {0: unit_scores(t * A_UNROLL)}
            for u in range(A_UNROLL):
                n = t * A_UNROLL + u
                if u + 1 < A_UNROLL:
                    st[u + 1] = unit_scores(n + 1)
                vt = vt_ext(n)
                emit(n, _normalise([_dot(v, jnp.exp2(sc).astype(BF16)) for v, sc in zip(vt, st.pop(u))]))
            return carry

        lax.fori_loop(0, units // A_UNROLL, body, 0)
        return

    st_ref, m_ref = scratch

    def scores(n, slot):
        sc = unit_scores(n)
        return _score_steps(st_ref, m_ref, slot, A_TK, 1, lambda hh, keys: sc[hh])

    def finish(n, slot):
        vt = vt_ext(n)
        return _softmax_pv_steps(st_ref, m_ref, slot, A_TK, 1, lambda hh, c: vt[hh], functools.partial(emit, n))

    _pipeline(units, st_ref.shape[0], A_AHEAD, scores, finish)


def _natten(q, k, vt, gate, slabs, slab_ids, bounded):
    bsz = q.shape[0]
    tok_spec = pl.BlockSpec((A_BATCH, SEQ, LANES), lambda p, g: (g, 0, p))
    scratch = [] if bounded else [pltpu.VMEM((A_SLOTS, 2, A_TK, A_TQ), F32),
                                  pltpu.VMEM((A_SLOTS, 2, 1, A_TQ), F32)]
    return pl.pallas_call(
        functools.partial(_natten_kernel, bounded=bounded),
        grid=(HEAD_PAIRS, bsz // A_BATCH),
        in_specs=[pl.BlockSpec(memory_space=pltpu.SMEM),
                  tok_spec, tok_spec,
                  pl.BlockSpec((A_BATCH, SEQ // A_TKC, LANES, A_TKC), lambda p, g: (g, 0, p, 0)),
                  tok_spec,
                  pl.BlockSpec((2,) + slabs.shape[1:], lambda p, g: (p, 0, 0, 0))],
        out_specs=tok_spec,
        out_shape=jax.ShapeDtypeStruct(q.shape, BF16),
        scratch_shapes=scratch,
        compiler_params=pltpu.CompilerParams(dimension_semantics=("parallel", "parallel"),
                                             vmem_limit_bytes=VMEM_LIMIT),
        name="natten_bounded" if bounded else "natten",
    )(slab_ids, q, k, vt, gate, slabs)


def _natten_bias(rpb):
    qc = np.arange(GRID_W)
    kc = np.arange(GRID_W)
    cs = np.clip(qc - WIN_COLS // 2, 0, GRID_W - WIN_COLS)
    col_ok = (kc[:, None] >= cs[None, :]) & (kc[:, None] < cs[None, :] + WIN_COLS)
    n_rel, n_col = 2 * WIN_ROWS - 1, 2 * WIN_COLS - 1
    dc = np.clip(kc[:, None] - qc[None, :] + WIN_COLS - 1, 0, n_col - 1)
    select = (np.arange(n_col)[:, None] == dc.reshape(1, -1)).astype(np.float32)
    table = jnp.dot(rpb.reshape(HEADS * n_rel, n_col), select, precision=lax.Precision.HIGHEST)
    table = jnp.where(col_ok[None, None], table.reshape(HEADS, n_rel, GRID_W, GRID_W), NEG_BIAS)
    table = jnp.concatenate([table, jnp.full((HEADS, 1, GRID_W, GRID_W), NEG_BIAS, F32)], axis=1)
    n_blocks = GRID_ROWS // A_QROWS
    pairs, ids = [], []
    for i in (0, 1, n_blocks - 1):
        ws = A_KCHUNK * min(max(i - 1, 0), n_blocks - A_NKC)
        for b in range(A_KROWS):
            for half in range(A_QROWS // 2):
                rel = []
                for a in (2 * half, 2 * half + 1):
                    qr = A_QROWS * i + a
                    rs = min(max(qr - WIN_ROWS // 2, 0), GRID_ROWS - WIN_ROWS)
                    rel.append(ws + b - qr + WIN_ROWS - 1 if rs <= ws + b < rs + WIN_ROWS else n_rel)
                if tuple(rel) not in pairs:
                    pairs.append(tuple(rel))
                ids.append(pairs.index(tuple(rel)))
    slabs = jnp.stack([jnp.concatenate([table[:, left], table[:, right]], axis=-1) for left, right in pairs], axis=1)
    return slabs, jnp.asarray(ids, jnp.int32)


def _gqa_kernel(q_ref, k_ref, vt_ref, gate_ref, o_ref, vt_ext_ref, *scratch, bounded):
    blocks, qgroup, _ = q_ref.shape
    units = blocks * (qgroup // B_TQ)
    n_chunks = SEQ // B_TKC

    @pl.when(pl.program_id(2) == 0)
    def _():
        for hh in range(2):
            vt_ext_ref[hh, :HEAD_DIM] = vt_ref[0, hh * HEAD_DIM:(hh + 1) * HEAD_DIM, :]
            vt_ext_ref[hh, HEAD_DIM:] = jnp.ones((LANES - HEAD_DIM, SEQ), BF16)

    masks = _head_masks()

    def rows(n):
        start = (n // blocks) * B_TQ
        return pl.ds(start if isinstance(n, int) else pl.multiple_of(start, B_TQ), B_TQ)

    def emit(n, ot):
        g = gate_ref[n % blocks, rows(n), :].astype(F32)
        o_ref[n % blocks, rows(n), :] = (ot.T * _silu(g)).astype(BF16)

    def masked_q(n):
        q2 = q_ref[n % blocks, rows(n), :]
        return [q2 * masks[hh] for hh in range(2)]

    def vt_chunk(hh, c):
        return vt_ext_ref[hh, :, c * B_TKC:(c + 1) * B_TKC]

    if bounded:
        def body(t, carry):
            items = [(t * B_UNROLL + u, c) for u in range(B_UNROLL) for c in range(n_chunks)]
            qm, st, accs = {}, {}, {}

            def qk(idx):
                n, c = items[idx]
                if c == 0:
                    qm[idx // n_chunks] = masked_q(n)
                for hh in range(2):
                    st[idx, hh] = _dot_nt(k_ref[0, c * B_TKC:(c + 1) * B_TKC, :], qm[idx // n_chunks][hh])

            qk(0)
            for idx, (n, c) in enumerate(items):
                if idx + 1 < len(items):
                    qk(idx + 1)
                for hh in range(2):
                    pv = _dot(vt_chunk(hh, c), jnp.exp2(st.pop((idx, hh))).astype(BF16))
                    accs[hh] = pv if c == 0 else accs[hh] + pv
                if c == n_chunks - 1:
                    emit(n, _normalise([accs[0], accs[1]]))
            return carry

        lax.fori_loop(0, units // B_UNROLL, body, 0)
        return

    st_ref, m_ref = scratch

    def scores(n, slot):
        qm = masked_q(n)
        return _score_steps(st_ref, m_ref, slot, B_TKC, n_chunks,
                            lambda hh, keys: _dot_nt(k_ref[0, keys, :], qm[hh]))

    def finish(n, slot):
        return _softmax_pv_steps(st_ref, m_ref, slot, B_TKC, n_chunks, vt_chunk, functools.partial(emit, n))

    _pipeline(units, st_ref.shape[0], B_AHEAD, scores, finish)


def _gqa(q, k, vt, gate, bounded):
    bsz = k.shape[0]
    blocks_per_pair = HEAD_PAIRS // (KV_HEADS_B // 2)
    groups = SEQ // B_QGROUP
    q_spec = pl.BlockSpec((blocks_per_pair, B_QGROUP, LANES), lambda b, j, g: (j, b * groups + g, 0))
    scratch = [pltpu.VMEM((2, LANES, SEQ), BF16)]
    if not bounded:
        scratch += [pltpu.VMEM((B_SLOTS, 2, SEQ, B_TQ), F32), pltpu.VMEM((B_SLOTS, 2, 1, B_TQ), F32)]
    return pl.pallas_call(
        functools.partial(_gqa_kernel, bounded=bounded),
        grid=(bsz, KV_HEADS_B // 2, groups),
        in_specs=[q_spec,
                  pl.BlockSpec((1, SEQ, LANES), lambda b, j, g: (b, 0, j)),
                  pl.BlockSpec((1, LANES, SEQ), lambda b, j, g: (b, j, 0)),
                  q_spec],
        out_specs=q_spec,
        out_shape=jax.ShapeDtypeStruct(q.shape, BF16),
        scratch_shapes=scratch,
        compiler_params=pltpu.CompilerParams(
            dimension_semantics=("parallel", "parallel", "arbitrary"),
            vmem_limit_bytes=VMEM_LIMIT),
        name="gqa_bounded" if bounded else "gqa",
    )(q, k, vt, gate)


def _pair_heads(w, axis):
    lead, trail = w.shape[:axis], w.shape[axis + 1:]
    group = HEADS // KV_HEADS_B
    w = w.reshape(*lead, KV_HEADS_B // 2, 2, group, HEAD_DIM, *trail)
    w = jnp.swapaxes(w, len(lead) + 1, len(lead) + 2)
    return w.reshape(*lead, D_MODEL, *trail)


def _rope_tables():
    t = np.arange(SEQ)
    pos = np.stack([t // GRID_W, t % GRID_W], axis=1).astype(np.float64)
    lane = np.arange(LANES) % HEAD_DIM
    section = lane // ROPE_SECTION
    n_freq = ROPE_SECTION // 2
    inv = np.power(ROPE_THETA, -np.arange(n_freq, dtype=np.float64) * 2.0 / ROPE_SECTION)
    ang = pos[:, section] * inv[lane % n_freq][None, :]
    sign = np.where(lane % ROPE_SECTION < n_freq, -1.0, 1.0)
    return jnp.asarray(np.cos(ang), F32), jnp.asarray(np.sin(ang) * sign[None, :], F32)


def kernel(x, p, norm_g, a_w_in, a_rpb, a_w_out, b_w_in, b_q_norm, b_k_norm, b_w_out, ple_norm_g, ple_w_gate,
           ple_w_proj, final_norm_g):
    bsz, seq, d = x.shape
    assert (seq, d) == (SEQ, D_MODEL)
    depth = p.shape[0]
    m = bsz * seq
    scale = math.log2(math.e) / math.sqrt(HEAD_DIM)
    xf = x.reshape(m, d)
    p_all = p.reshape(depth, m, PLE_DIM)
    cos, sin = _rope_tables()
    seg4 = np.kron(np.eye(4), np.ones((HEAD_DIM, HEAD_DIM)))
    seg = jnp.asarray(np.concatenate([seg4, seg4], axis=0), BF16)
    q_col_scale = np.ones((4 * D_MODEL,), np.float32)
    q_col_scale[:D_MODEL] = scale
    row2 = lambda v: v.reshape(1, -1)

    for i in range(depth):
        j = i // NUM_MIXERS
        if i % NUM_MIXERS == 0:
            w = (a_w_in[j] * q_col_scale).astype(BF16)
            q, k, vt, gate, n2 = _inproj_a(xf, row2(norm_g[i]), w, jnp.asarray(seg4, BF16))
            q, k, gate = (t.reshape(bsz, seq, D_MODEL) for t in (q, k, gate))
            rpb = a_rpb[j] * math.log2(math.e)
            slabs, slab_ids = _natten_bias(rpb)
            n2 = jnp.max(n2, axis=0)
            score_bound = 1.02 * jnp.sqrt(jnp.max(n2[0] * n2[1])) + jnp.max(jnp.abs(rpb))
            a = lax.cond((score_bound <= SCORE_BOUND) & (jnp.max(n2[2]) <= VALUE_BOUND),
                         lambda: _natten(q, k, vt, gate, slabs, slab_ids, bounded=True),
                         lambda: _natten(q, k, vt, gate, slabs, slab_ids, bounded=False)).reshape(m, D_MODEL)
            wo = a_w_out[j].astype(BF16)
        else:
            w = b_w_in[j]
            qk_w = D_MODEL + KV_W_B
            w = jnp.concatenate([_pair_heads(w[:, :D_MODEL], 1), w[:, D_MODEL:qk_w + KV_W_B],
                                 _pair_heads(w[:, qk_w + KV_W_B:], 1)], axis=1).astype(BF16)
            head_gain = jnp.concatenate([jnp.tile(b_q_norm[j] * scale, HEADS), jnp.tile(b_k_norm[j], KV_HEADS_B)])
            q, k, vt, gate, v_max = _inproj_b(xf, row2(norm_g[i]), w, row2(head_gain), cos, sin, seg)
            score_bound = HEAD_DIM * jnp.max(jnp.abs(b_q_norm[j] * scale)) * jnp.max(jnp.abs(b_k_norm[j]))
            k3 = k.reshape(bsz, seq, KV_W_B)
            a = lax.cond((score_bound <= SCORE_BOUND) & (jnp.max(v_max) <= VALUE_BOUND),
                         lambda: _gqa(q, k3, vt, gate, bounded=True),
                         lambda: _gqa(q, k3, vt, gate, bounded=False))
            wo = _pair_heads(b_w_out[j], 0).astype(BF16)
        xf = _outproj(a, xf, p_all, i, wo, row2(ple_norm_g[i]), ple_w_gate[i].astype(BF16),
                      ple_w_proj[i].astype(BF16), row2(final_norm_g), final_norm=(i == depth - 1))
    return xf.reshape(bsz, seq, d)
```

```python
import functools
import math

import jax
import jax.numpy as jnp
import numpy as np
from jax import lax
from jax.experimental import pallas as pl
from jax.experimental.pallas import tpu as pltpu

D_MODEL = 1024
SEQ = 4096
GRID_W = 64
GRID_ROWS = SEQ // GRID_W
HEAD_DIM = 64
HEADS = D_MODEL // HEAD_DIM
KV_HEADS_B = 4
KV_W_B = KV_HEADS_B * HEAD_DIM
WIN_ROWS = 8
WIN_COLS = 16
ROPE_THETA = 10000.0
ROPE_SECTION = HEAD_DIM // 2
PLE_DIM = 256
NORM_EPS = 1e-6
NUM_MIXERS = 2

LANES = 128
HEAD_PAIRS = HEADS // 2
NEG_BIAS = -1e30

ROW_TILE = 1024
A_QROWS = 4
A_KROWS = 12
A_KCHUNK = 4
A_TQ = A_QROWS * GRID_W
A_TK = A_KROWS * GRID_W
A_TKC = A_KCHUNK * GRID_W
A_NKC = A_KROWS // A_KCHUNK
A_BATCH = 2
A_UNROLL = 16
A_SLOTS, A_AHEAD = 4, 2
B_SLOTS, B_AHEAD = 2, 1
B_UNROLL = 8
SCORE_BOUND = 80.0
VALUE_BOUND = 2.0 ** 30
NORM_ROW_GROUPS = 4
B_TQ = 256
B_QGROUP = 4096
B_TKC = 4096
VMEM_LIMIT = 56 * 1024 * 1024

BF16 = jnp.bfloat16
F32 = jnp.float32


def _rms(x, g):
    ms = jnp.mean(x * x, axis=-1, keepdims=True)
    return x * lax.rsqrt(ms + NORM_EPS) * g


def _dot(a, b):
    return jnp.dot(a, b, preferred_element_type=F32)


def _dot_nt(a, b):
    return lax.dot_general(a, b, (((1,), (1,)), ((), ())), preferred_element_type=F32)


def _head_masks():
    lo = (lax.broadcasted_iota(jnp.int32, (1, LANES), 1) < HEAD_DIM).astype(BF16)
    return lo, 1 - lo


def _silu(g):
    return g * jax.nn.sigmoid(g)


def _score_steps(st_ref, m_ref, slot, key_chunk, n_chunks, chunk_scores):
    maxes = ([], [])

    def chunk(c):
        keys = pl.ds(c * key_chunk, key_chunk)
        for hh in range(2):
            st = chunk_scores(hh, keys)
            st_ref[slot, hh, keys, :] = st
            maxes[hh].append(jnp.max(st, axis=0, keepdims=True))

    def done():
        for hh in range(2):
            m_ref[slot, hh] = functools.reduce(jnp.maximum, maxes[hh])

    return [functools.partial(chunk, c) for c in range(n_chunks)] + [done]


def _normalise(accs):
    return jnp.concatenate([a[:HEAD_DIM] / a[HEAD_DIM:HEAD_DIM + 1] for a in accs], axis=0)


def _softmax_pv_steps(st_ref, m_ref, slot, key_chunk, n_chunks, vt_chunk, emit):
    accs = [None, None]

    def chunk(c):
        keys = pl.ds(c * key_chunk, key_chunk)
        for hh in range(2):
            pt = jnp.exp2(st_ref[slot, hh, keys, :] - m_ref[slot, hh]).astype(BF16)
            pv = _dot(vt_chunk(hh, c), pt)
            accs[hh] = pv if accs[hh] is None else accs[hh] + pv

    def done():
        emit(_normalise(accs))

    return [functools.partial(chunk, c) for c in range(n_chunks)] + [done]


def _pipeline(units, slots, ahead, scores, finish):
    assert units % slots == 0 and 0 < ahead < slots

    def run(*step_lists):
        for steps in zip(*step_lists):
            for step in reversed(steps):
                step()

    def trip(base, last):
        for k in range(slots):
            if not last or k + ahead < slots:
                run(scores(base + k + ahead, (k + ahead) % slots), finish(base + k, k))
            else:
                run(finish(base + k, k))

    for n in range(ahead):
        run(scores(n, n))

    def body(t, carry):
        trip(t * slots, False)
        return carry

    lax.fori_loop(0, units // slots - 1, body, 0)
    trip(units - slots, True)


def _inproj_a_kernel(x_ref, g_ref, w_ref, seg_ref, q_ref, k_ref, vt_ref, gate_ref, n2_ref):
    h = _rms(x_ref[...], g_ref[...]).astype(BF16)
    wide = 2 * LANES
    head_norms = []
    for c, o_ref in enumerate((q_ref, k_ref)):
        y = _dot(h, w_ref[:, c * D_MODEL:(c + 1) * D_MODEL])
        o_ref[...] = y.astype(BF16)
        ss = y * y
        rows = ss.shape[0] // NORM_ROW_GROUPS
        ss = functools.reduce(jnp.maximum, [ss[r * rows:(r + 1) * rows] for r in range(NORM_ROW_GROUPS)]).astype(BF16)
        n2 = jnp.concatenate([_dot(ss[:, b * wide:(b + 1) * wide], seg_ref[...]) for b in range(D_MODEL // wide)],
                             axis=1)
        head_norms.append(jnp.max(n2, axis=0, keepdims=True))
    gate_ref[...] = _dot(h, w_ref[:, 3 * D_MODEL:]).astype(BF16)
    vt = lax.dot_general(w_ref[:, 2 * D_MODEL:3 * D_MODEL], h, (((0,), (1,)), ((), ())),
                         preferred_element_type=F32)
    v_max = jnp.max(jnp.max(jnp.abs(vt), axis=0, keepdims=True), axis=1, keepdims=True)
    n2_ref[0] = jnp.concatenate(head_norms + [jnp.broadcast_to(v_max, (1, D_MODEL))], axis=0)
    vt = vt.astype(BF16)
    for c in range(vt_ref.shape[1]):
        vt_ref[0, c] = vt[:, c * A_TKC:(c + 1) * A_TKC]


def _inproj_a(x, g, w, seg):
    m = x.shape[0]
    out = jax.ShapeDtypeStruct((m, D_MODEL), BF16)
    tiles = SEQ // ROW_TILE
    fixed = lambda i: (0, 0)
    row_spec = pl.BlockSpec((ROW_TILE, D_MODEL), lambda i: (i, 0))
    return pl.pallas_call(
        _inproj_a_kernel,
        grid=(m // ROW_TILE,),
        in_specs=[row_spec,
                  pl.BlockSpec((1, D_MODEL), fixed),
                  pl.BlockSpec((D_MODEL, 4 * D_MODEL), fixed),
                  pl.BlockSpec((2 * LANES, 2 * LANES), fixed)],
        out_specs=[row_spec, row_spec,
                   pl.BlockSpec((1, ROW_TILE // A_TKC, D_MODEL, A_TKC), lambda i: (i // tiles, i % tiles, 0, 0)),
                   row_spec,
                   pl.BlockSpec((1, 3, D_MODEL), lambda i: (i, 0, 0))],
        out_shape=[out, out, jax.ShapeDtypeStruct((m // SEQ, SEQ // A_TKC, D_MODEL, A_TKC), BF16), out,
                   jax.ShapeDtypeStruct((m // ROW_TILE, 3, D_MODEL), F32)],
        compiler_params=pltpu.CompilerParams(dimension_semantics=("parallel",), vmem_limit_bytes=VMEM_LIMIT),
        name="inproj_a",
    )(x, g, w, seg)


def _inproj_b_kernel(x_ref, g_ref, w_ref, hg_ref, cos_ref, sin_ref, seg_ref, q_ref, k_ref, vt_ref, gate_ref,
                     vmax_ref):
    qk_w = D_MODEL + KV_W_B
    h = _rms(x_ref[...], g_ref[...]).astype(BF16)
    seg = seg_ref[...]
    cos = cos_ref[...]
    sin = sin_ref[...]
    first_half = (lax.broadcasted_iota(jnp.int32, (1, LANES), 1) % ROPE_SECTION) < (ROPE_SECTION // 2)
    yqk = _dot(h, w_ref[:, :qk_w])
    wide = 2 * LANES
    for pb in range(qk_w // wide):
        y = yqk[:, pb * wide:(pb + 1) * wide]
        ss = y * y
        hi = ss.astype(BF16)
        lo = (ss - hi.astype(F32)).astype(BF16)
        ms = _dot(jnp.concatenate([hi, lo], axis=1), seg) * (1.0 / HEAD_DIM)
        yn2 = y * lax.rsqrt(ms + NORM_EPS) * hg_ref[:, pb * wide:(pb + 1) * wide]
        for half in range(2):
            c = 2 * pb + half
            yn = yn2[:, half * LANES:(half + 1) * LANES]
            partner = jnp.where(first_half, pltpu.roll(yn, LANES - ROPE_SECTION // 2, 1),
                                pltpu.roll(yn, ROPE_SECTION // 2, 1))
            r = (yn * cos + partner * sin).astype(BF16)
            if c < D_MODEL // LANES:
                q_ref[c] = r
            else:
                k_ref[:, (c - D_MODEL // LANES) * LANES:(c - D_MODEL // LANES + 1) * LANES] = r
    vt = lax.dot_general(w_ref[:, qk_w:qk_w + KV_W_B], h, (((0,), (1,)), ((), ())),
                         preferred_element_type=F32)
    vt_ref[0] = vt.astype(BF16)
    v_max = jnp.max(jnp.max(jnp.abs(vt), axis=0, keepdims=True), axis=1, keepdims=True)
    vmax_ref[0] = jnp.broadcast_to(v_max, (1, LANES))
    gate = _dot(h, w_ref[:, qk_w + KV_W_B:])
    for c in range(D_MODEL // LANES):
        gate_ref[c] = gate[:, c * LANES:(c + 1) * LANES].astype(BF16)


def _inproj_b(x, g, w, head_gain, cos, sin, seg):
    m = x.shape[0]
    n_w = 2 * D_MODEL + 2 * KV_W_B
    tiles = SEQ // ROW_TILE
    row = lambda i: (i, 0)
    fixed = lambda i: (0, 0)
    pos = lambda i: (i % tiles, 0)
    wide = jax.ShapeDtypeStruct((D_MODEL // LANES, m, LANES), BF16)
    wide_spec = pl.BlockSpec((D_MODEL // LANES, ROW_TILE, LANES), lambda i: (0, i, 0))
    return pl.pallas_call(
        _inproj_b_kernel,
        grid=(m // ROW_TILE,),
        in_specs=[pl.BlockSpec((ROW_TILE, D_MODEL), row),
                  pl.BlockSpec((1, D_MODEL), fixed),
                  pl.BlockSpec((D_MODEL, n_w), fixed),
                  pl.BlockSpec((1, D_MODEL + KV_W_B), fixed),
                  pl.BlockSpec((ROW_TILE, LANES), pos),
                  pl.BlockSpec((ROW_TILE, LANES), pos),
                  pl.BlockSpec((4 * LANES, 2 * LANES), fixed)],
        out_specs=[wide_spec,
                   pl.BlockSpec((ROW_TILE, KV_W_B), row),
                   pl.BlockSpec((1, KV_W_B, ROW_TILE), lambda i: (i // tiles, 0, i % tiles)),
                   wide_spec,
                   pl.BlockSpec((1, 1, LANES), lambda i: (i, 0, 0))],
        out_shape=[wide, jax.ShapeDtypeStruct((m, KV_W_B), BF16),
                   jax.ShapeDtypeStruct((m // SEQ, KV_W_B, SEQ), BF16), wide,
                   jax.ShapeDtypeStruct((m // ROW_TILE, 1, LANES), F32)],
        compiler_params=pltpu.CompilerParams(dimension_semantics=("parallel",), vmem_limit_bytes=VMEM_LIMIT),
        name="inproj_b",
    )(x, g, w, head_gain, cos, sin, seg)


def _outproj_kernel(a_ref, x_ref, p_ref, wo_ref, pg_ref, wg_ref, wp_ref, fg_ref, o_ref, *, final_norm):
    if len(a_ref.shape) == 3:
        a = jnp.concatenate([a_ref[c] for c in range(a_ref.shape[0])], axis=1)
    else:
        a = a_ref[...]
    x1 = x_ref[...] + _dot(a, wo_ref[...])
    hn = _rms(x1, pg_ref[...]).astype(BF16)
    gate = jax.nn.sigmoid(_dot(hn, wg_ref[...]))
    x2 = x1 + gate * _dot(p_ref[...].astype(BF16), wp_ref[...])
    o_ref[...] = _rms(x2, fg_ref[...]) if final_norm else x2


def _outproj(a, x, p_all, layer, wo, pg, wg, wp, fg, final_norm):
    m = x.shape[0]
    row = lambda i: (i, 0)
    fixed = lambda i: (0, 0)
    return pl.pallas_call(
        functools.partial(_outproj_kernel, final_norm=final_norm),
        grid=(m // ROW_TILE,),
        in_specs=[pl.BlockSpec((ROW_TILE, D_MODEL), row) if a.ndim == 2 else
                  pl.BlockSpec((a.shape[0], ROW_TILE, LANES), lambda i: (0, i, 0)),
                  pl.BlockSpec((ROW_TILE, D_MODEL), row),
                  pl.BlockSpec((None, ROW_TILE, PLE_DIM), lambda i: (layer, i, 0)),
                  pl.BlockSpec((D_MODEL, D_MODEL), fixed),
                  pl.BlockSpec((1, D_MODEL), fixed),
                  pl.BlockSpec((D_MODEL, D_MODEL), fixed),
                  pl.BlockSpec((PLE_DIM, D_MODEL), fixed),
                  pl.BlockSpec((1, D_MODEL), fixed)],
        out_specs=pl.BlockSpec((ROW_TILE, D_MODEL), row),
        out_shape=jax.ShapeDtypeStruct((m, D_MODEL), F32),
        compiler_params=pltpu.CompilerParams(dimension_semantics=("parallel",), vmem_limit_bytes=VMEM_LIMIT),
        name="outproj_final" if final_norm else "outproj",
    )(a, x, p_all, wo, pg, wg, wp, fg)


def _natten_kernel(slab_ids_ref, q_ref, k_ref, vt_ref, gate_ref, bias_ref, o_ref, st_ref, m_ref):
    n_blocks = GRID_ROWS // A_QROWS
    masks = _head_masks()
    ones = jnp.ones((LANES - HEAD_DIM, A_TK), BF16)

    def where(n):
        n = jnp.asarray(n, jnp.int32)
        bb, i = n // n_blocks, n % n_blocks
        first_chunk = jnp.clip(i - 1, 0, n_blocks - A_NKC)
        variant = jnp.where(i == 0, 0, jnp.where(i == n_blocks - 1, 2, 1))
        return bb, pl.ds(pl.multiple_of(i * A_TQ, A_TQ), A_TQ), first_chunk, variant

    def unit_scores(n):
        bb, q_rows, first_chunk, variant = where(n)
        q2 = q_ref[bb, q_rows, :]
        kwin = k_ref[bb, pl.ds(pl.multiple_of(first_chunk * A_TKC, A_TKC), A_TK), :]
        out = []
        for hh in range(2):
            bias = jnp.concatenate(
                [jnp.concatenate([bias_ref[hh, slab_ids_ref[(variant * A_KROWS + r) * 2 + half]]
                                  for half in range(2)], axis=1) for r in range(A_KROWS)], axis=0)
            out.append(_dot_nt(kwin, q2 * masks[hh]) + bias)
        return out

    def vt_ext(n):
        bb, _, first_chunk, _ = where(n)
        vt = jnp.concatenate([vt_ref[bb, first_chunk + kc] for kc in range(A_NKC)], axis=1)
        return [jnp.concatenate([vt[hh * HEAD_DIM:(hh + 1) * HEAD_DIM], ones], axis=0) for hh in range(2)]

    def emit(n, ot):
        bb, q_rows, _, _ = where(n)
        o_ref[bb, q_rows, :] = (ot.T * _silu(gate_ref[bb, q_rows, :].astype(F32))).astype(BF16)

    units = q_ref.shape[0] * n_blocks
    bounded = slab_ids_ref[slab_ids_ref.shape[0] - 1] == 1

    @pl.when(bounded)
    def _():
        def body(t, carry):
            st = {0: unit_scores(t * A_UNROLL)}
            for u in range(A_UNROLL):
                n = t * A_UNROLL + u
                if u + 1 < A_UNROLL:
                    st[u + 1] = unit_scores(n + 1)
                vt = vt_ext(n)
                emit(n, _normalise([_dot(v, jnp.exp2(sc).astype(BF16)) for v, sc in zip(vt, st.pop(u))]))
            return carry

        lax.fori_loop(0, units // A_UNROLL, body, 0)

    @pl.when(jnp.logical_not(bounded))
    def _():
        def scores(n, slot):
            sc = unit_scores(n)
            return _score_steps(st_ref, m_ref, slot, A_TK, 1, lambda hh, keys: sc[hh])

        def finish(n, slot):
            vt = vt_ext(n)
            return _softmax_pv_steps(st_ref, m_ref, slot, A_TK, 1, lambda hh, c: vt[hh], functools.partial(emit, n))

        _pipeline(units, st_ref.shape[0], A_AHEAD, scores, finish)


def _natten(q, k, vt, gate, slabs, slab_ids):
    bsz = q.shape[0]
    tok_spec = pl.BlockSpec((A_BATCH, SEQ, LANES), lambda p, g: (g, 0, p))
    return pl.pallas_call(
        _natten_kernel,
        grid=(HEAD_PAIRS, bsz // A_BATCH),
        in_specs=[pl.BlockSpec(memory_space=pltpu.SMEM),
                  tok_spec, tok_spec,
                  pl.BlockSpec((A_BATCH, SEQ // A_TKC, LANES, A_TKC), lambda p, g: (g, 0, p, 0)),
                  tok_spec,
                  pl.BlockSpec((2,) + slabs.shape[1:], lambda p, g: (p, 0, 0, 0))],
        out_specs=tok_spec,
        out_shape=jax.ShapeDtypeStruct(q.shape, BF16),
        scratch_shapes=[pltpu.VMEM((A_SLOTS, 2, A_TK, A_TQ), F32), pltpu.VMEM((A_SLOTS, 2, 1, A_TQ), F32)],
        compiler_params=pltpu.CompilerParams(dimension_semantics=("parallel", "parallel"),
                                             vmem_limit_bytes=VMEM_LIMIT),
        name="natten",
    )(slab_ids, q, k, vt, gate, slabs)


def _natten_bias(rpb):
    qc = np.arange(GRID_W)
    kc = np.arange(GRID_W)
    cs = np.clip(qc - WIN_COLS // 2, 0, GRID_W - WIN_COLS)
    col_ok = (kc[:, None] >= cs[None, :]) & (kc[:, None] < cs[None, :] + WIN_COLS)
    n_rel, n_col = 2 * WIN_ROWS - 1, 2 * WIN_COLS - 1
    dc = np.clip(kc[:, None] - qc[None, :] + WIN_COLS - 1, 0, n_col - 1)
    select = (np.arange(n_col)[:, None] == dc.reshape(1, -1)).astype(np.float32)
    table = jnp.dot(rpb.reshape(HEADS * n_rel, n_col), select, precision=lax.Precision.HIGHEST)
    table = jnp.where(col_ok[None, None], table.reshape(HEADS, n_rel, GRID_W, GRID_W), NEG_BIAS)
    table = jnp.concatenate([table, jnp.full((HEADS, 1, GRID_W, GRID_W), NEG_BIAS, F32)], axis=1)
    n_blocks = GRID_ROWS // A_QROWS
    pairs, ids = [], []
    for i in (0, 1, n_blocks - 1):
        ws = A_KCHUNK * min(max(i - 1, 0), n_blocks - A_NKC)
        for b in range(A_KROWS):
            for half in range(A_QROWS // 2):
                rel = []
                for a in (2 * half, 2 * half + 1):
                    qr = A_QROWS * i + a
                    rs = min(max(qr - WIN_ROWS // 2, 0), GRID_ROWS - WIN_ROWS)
                    rel.append(ws + b - qr + WIN_ROWS - 1 if rs <= ws + b < rs + WIN_ROWS else n_rel)
                if tuple(rel) not in pairs:
                    pairs.append(tuple(rel))
                ids.append(pairs.index(tuple(rel)))
    slabs = jnp.stack([jnp.concatenate([table[:, left], table[:, right]], axis=-1) for left, right in pairs], axis=1)
    return slabs, jnp.asarray(ids, jnp.int32)


def _gqa_kernel(q_ref, k_ref, vt_ref, gate_ref, o_ref, vt_ext_ref, *scratch, bounded):
    blocks, qgroup, _ = q_ref.shape
    units = blocks * (qgroup // B_TQ)
    n_chunks = SEQ // B_TKC

    @pl.when(pl.program_id(2) == 0)
    def _():
        for hh in range(2):
            vt_ext_ref[hh, :HEAD_DIM] = vt_ref[0, hh * HEAD_DIM:(hh + 1) * HEAD_DIM, :]
            vt_ext_ref[hh, HEAD_DIM:] = jnp.ones((LANES - HEAD_DIM, SEQ), BF16)

    masks = _head_masks()

    def rows(n):
        start = (n // blocks) * B_TQ
        return pl.ds(start if isinstance(n, int) else pl.multiple_of(start, B_TQ), B_TQ)

    def emit(n, ot):
        g = gate_ref[n % blocks, rows(n), :].astype(F32)
        o_ref[n % blocks, rows(n), :] = (ot.T * _silu(g)).astype(BF16)

    def masked_q(n):
        q2 = q_ref[n % blocks, rows(n), :]
        return [q2 * masks[hh] for hh in range(2)]

    def vt_chunk(hh, c):
        return vt_ext_ref[hh, :, c * B_TKC:(c + 1) * B_TKC]

    if bounded:
        def body(t, carry):
            items = [(t * B_UNROLL + u, c) for u in range(B_UNROLL) for c in range(n_chunks)]
            qm, st, accs = {}, {}, {}

            def qk(idx):
                n, c = items[idx]
                if c == 0:
                    qm[idx // n_chunks] = masked_q(n)
                for hh in range(2):
                    st[idx, hh] = _dot_nt(k_ref[0, c * B_TKC:(c + 1) * B_TKC, :], qm[idx // n_chunks][hh])

            qk(0)
            for idx, (n, c) in enumerate(items):
                if idx + 1 < len(items):
                    qk(idx + 1)
                for hh in range(2):
                    pv = _dot(vt_chunk(hh, c), jnp.exp2(st.pop((idx, hh))).astype(BF16))
                    accs[hh] = pv if c == 0 else accs[hh] + pv
                if c == n_chunks - 1:
                    emit(n, _normalise([accs[0], accs[1]]))
            return carry

        lax.fori_loop(0, units // B_UNROLL, body, 0)
        return

    st_ref, m_ref = scratch

    def scores(n, slot):
        qm = masked_q(n)
        return _score_steps(st_ref, m_ref, slot, B_TKC, n_chunks,
                            lambda hh, keys: _dot_nt(k_ref[0, keys, :], qm[hh]))

    def finish(n, slot):
        return _softmax_pv_steps(st_ref, m_ref, slot, B_TKC, n_chunks, vt_chunk, functools.partial(emit, n))

    _pipeline(units, st_ref.shape[0], B_AHEAD, scores, finish)


def _gqa(q, k, vt, gate, bounded):
    bsz = k.shape[0]
    blocks_per_pair = HEAD_PAIRS // (KV_HEADS_B // 2)
    groups = SEQ // B_QGROUP
    q_spec = pl.BlockSpec((blocks_per_pair, B_QGROUP, LANES), lambda b, j, g: (j, b * groups + g, 0))
    scratch = [pltpu.VMEM((2, LANES, SEQ), BF16)]
    if not bounded:
        scratch += [pltpu.VMEM((B_SLOTS, 2, SEQ, B_TQ), F32), pltpu.VMEM((B_SLOTS, 2, 1, B_TQ), F32)]
    return pl.pallas_call(
        functools.partial(_gqa_kernel, bounded=bounded),
        grid=(bsz, KV_HEADS_B // 2, groups),
        in_specs=[q_spec,
                  pl.BlockSpec((1, SEQ, LANES), lambda b, j, g: (b, 0, j)),
                  pl.BlockSpec((1, LANES, SEQ), lambda b, j, g: (b, j, 0)),
                  q_spec],
        out_specs=q_spec,
        out_shape=jax.ShapeDtypeStruct(q.shape, BF16),
        scratch_shapes=scratch,
        compiler_params=pltpu.CompilerParams(
            dimension_semantics=("parallel", "parallel", "arbitrary"),
            vmem_limit_bytes=VMEM_LIMIT),
        name="gqa_bounded" if bounded else "gqa",
    )(q, k, vt, gate)


def _pair_heads(w, axis):
    lead, trail = w.shape[:axis], w.shape[axis + 1:]
    group = HEADS // KV_HEADS_B
    w = w.reshape(*lead, KV_HEADS_B // 2, 2, group, HEAD_DIM, *trail)
    w = jnp.swapaxes(w, len(lead) + 1, len(lead) + 2)
    return w.reshape(*lead, D_MODEL, *trail)


def _rope_tables():
    t = np.arange(SEQ)
    pos = np.stack([t // GRID_W, t % GRID_W], axis=1).astype(np.float64)
    lane = np.arange(LANES) % HEAD_DIM
    section = lane // ROPE_SECTION
    n_freq = ROPE_SECTION // 2
    inv = np.power(ROPE_THETA, -np.arange(n_freq, dtype=np.float64) * 2.0 / ROPE_SECTION)
    ang = pos[:, section] * inv[lane % n_freq][None, :]
    sign = np.where(lane % ROPE_SECTION < n_freq, -1.0, 1.0)
    return jnp.asarray(np.cos(ang), F32), jnp.asarray(np.sin(ang) * sign[None, :], F32)


def kernel(x, p, norm_g, a_w_in, a_rpb, a_w_out, b_w_in, b_q_norm, b_k_norm, b_w_out, ple_norm_g, ple_w_gate,
           ple_w_proj, final_norm_g):
    bsz, seq, d = x.shape
    assert (seq, d) == (SEQ, D_MODEL)
    depth = p.shape[0]
    m = bsz * seq
    scale = math.log2(math.e) / math.sqrt(HEAD_DIM)
    xf = x.reshape(m, d)
    p_all = p.reshape(depth, m, PLE_DIM)
    cos, sin = _rope_tables()
    seg4 = np.kron(np.eye(4), np.ones((HEAD_DIM, HEAD_DIM)))
    seg = jnp.asarray(np.concatenate([seg4, seg4], axis=0), BF16)
    q_col_scale = np.ones((4 * D_MODEL,), np.float32)
    q_col_scale[:D_MODEL] = scale
    row2 = lambda v: v.reshape(1, -1)

    for i in range(depth):
        j = i // NUM_MIXERS
        if i % NUM_MIXERS == 0:
            w = (a_w_in[j] * q_col_scale).astype(BF16)
            q, k, vt, gate, n2 = _inproj_a(xf, row2(norm_g[i]), w, jnp.asarray(seg4, BF16))
            q, k, gate = (t.reshape(bsz, seq, D_MODEL) for t in (q, k, gate))
            rpb = a_rpb[j] * math.log2(math.e)
            slabs, slab_ids = _natten_bias(rpb)
            n2 = jnp.max(n2, axis=0)
            score_bound = 1.02 * jnp.sqrt(jnp.max(n2[0] * n2[1])) + jnp.max(jnp.abs(rpb))
            bounded = (score_bound <= SCORE_BOUND) & (jnp.max(n2[2]) <= VALUE_BOUND)
            slab_ids = jnp.concatenate([slab_ids, bounded.astype(jnp.int32).reshape(1)])
            a = _natten(q, k, vt, gate, slabs, slab_ids).reshape(m, D_MODEL)
            wo = a_w_out[j].astype(BF16)
        else:
            w = b_w_in[j]
            qk_w = D_MODEL + KV_W_B
            w = jnp.concatenate([_pair_heads(w[:, :D_MODEL], 1), w[:, D_MODEL:qk_w + KV_W_B],
                                 _pair_heads(w[:, qk_w + KV_W_B:], 1)], axis=1).astype(BF16)
            head_gain = jnp.concatenate([jnp.tile(b_q_norm[j] * scale, HEADS), jnp.tile(b_k_norm[j], KV_HEADS_B)])
            q, k, vt, gate, v_max = _inproj_b(xf, row2(norm_g[i]), w, row2(head_gain), cos, sin, seg)
            score_bound = HEAD_DIM * jnp.max(jnp.abs(b_q_norm[j] * scale)) * jnp.max(jnp.abs(b_k_norm[j]))
            k3 = k.reshape(bsz, seq, KV_W_B)
            a = lax.cond((score_bound <= SCORE_BOUND) & (jnp.max(v_max) <= VALUE_BOUND),
                         lambda: _gqa(q, k3, vt, gate, bounded=True),
                         lambda: _gqa(q, k3, vt, gate, bounded=False))
            wo = _pair_heads(b_w_out[j], 0).astype(BF16)
        xf = _outproj(a, xf, p_all, i, wo, row2(ple_norm_g[i]), ple_w_gate[i].astype(BF16),
                      ple_w_proj[i].astype(BF16), row2(final_norm_g), final_norm=(i == depth - 1))
    return xf.reshape(bsz, seq, d)
```
